```python
import math
import jax
import jax.numpy as jnp
from jax import lax
import numpy as np

D_MODEL = 1024
BATCH = 16
SEQ = 256
DEPTH = 4
DEC_BATCH = 8
DEC_SEQ = 1024
PAST_LEN = 512

GRID_W = 64
HEAD_DIM = 64
BRANCH_WIDTH = 512
N_BRANCH = 4
A_HEADS = BRANCH_WIDTH // HEAD_DIM
A_KV_HEADS = 2
A_GROUP = A_HEADS // A_KV_HEADS
A_WINDOW = 128
A_BLOCK = 128
Q_BLOCK = 128
B_HEADS = BRANCH_WIDTH // HEAD_DIM
DELTA_CHUNK = 64
CONV_K = 5
C_HEADS = BRANCH_WIDTH // HEAD_DIM
HGRN_CHUNK = 16
D_HEADS = BRANCH_WIDTH // HEAD_DIM
NH_ROWS = 8
NH_COLS = 16
NH_QCOLS = 16
NH_KCOLS = 32
D_FF = 4 * D_MODEL
ROPE_BASE = 10000.0
ATTN_SCALE = HEAD_DIM ** -0.5
EPS = 1e-6
NEG = -1e30
IN_SPLITS = (
    A_HEADS * HEAD_DIM, A_KV_HEADS * HEAD_DIM, A_KV_HEADS * HEAD_DIM,
    BRANCH_WIDTH, BRANCH_WIDTH, BRANCH_WIDTH, BRANCH_WIDTH, 2 * B_HEADS, 2 * B_HEADS,
    BRANCH_WIDTH, 2 * BRANCH_WIDTH, BRANCH_WIDTH, BRANCH_WIDTH,
    BRANCH_WIDTH, BRANCH_WIDTH, BRANCH_WIDTH,
    N_BRANCH * D_MODEL,
)
N_IN = sum(IN_SPLITS)

kernel_name = 'hybrid_flow_backbone_step'


def rmsnorm(x, w):
    xf = x.astype(jnp.float32)
    y = xf * lax.rsqrt(jnp.mean(xf * xf, axis=-1, keepdims=True) + EPS)
    return (y * w.astype(jnp.float32)).astype(x.dtype)


def l2norm(x):
    xf = x.astype(jnp.float32)
    return xf * lax.rsqrt(jnp.sum(xf * xf, axis=-1, keepdims=True) + EPS)


def adaln(cvec, w, b):
    return jax.nn.silu(cvec) @ w + b


def flip(t):
    return jnp.flip(t, axis=1)


def rope_2d(x):
    L = x.shape[1]
    t = jnp.arange(L)
    half = HEAD_DIM // 2
    quarter = half // 2
    inv = ROPE_BASE ** (-jnp.arange(quarter, dtype=jnp.float32) / quarter)

    def rot(xa, pos):
        ang = pos.astype(jnp.float32)[:, None] * inv[None, :]
        cos = jnp.cos(ang)[None, :, None, :].astype(x.dtype)
        sin = jnp.sin(ang)[None, :, None, :].astype(x.dtype)
        x1, x2 = xa[..., :quarter], xa[..., quarter:]
        return jnp.concatenate([x1 * cos - x2 * sin, x2 * cos + x1 * sin], axis=-1)

    return jnp.concatenate([rot(x[..., :half], t // GRID_W), rot(x[..., half:], t % GRID_W)], axis=-1)


def softmax_with_sink(s, sink):
    m = jnp.max(s, axis=-1, keepdims=True)
    if sink is not None:
        m = jnp.maximum(m, sink)
    p = jnp.exp(s - m)
    den = jnp.sum(p, axis=-1, keepdims=True)
    if sink is not None:
        den = den + jnp.exp(sink - m)
    return p / den


def dense_attn(q, k, v, sink):
    b, lq, nkv, ng, dh = q.shape
    nb = lq // Q_BLOCK
    qb = jnp.moveaxis(q.reshape(b, nb, Q_BLOCK, nkv, ng, dh), 1, 0)
    sink_b = None if sink is None else sink.astype(jnp.float32)[None, :, :, None, None]

    def block(qi):
        s = jnp.einsum('bqkgd,bskd->bkgqs', qi, k, preferred_element_type=jnp.float32) * ATTN_SCALE
        p = softmax_with_sink(s, sink_b)
        return jnp.einsum('bkgqs,bskd->bqkgd', p.astype(v.dtype), v)

    o = lax.map(block, qb)
    return jnp.moveaxis(o, 0, 1).reshape(b, lq, nkv, ng, dh)


def window_attn_latent(q, k, v, ck, cv, sink):
    b, L, nkv, ng, dh = q.shape
    nb = L // A_BLOCK
    pad = ((0, 0), (A_BLOCK, A_BLOCK), (0, 0), (0, 0))
    kp = jnp.pad(k, pad)
    vp = jnp.pad(v, pad)
    sink_b = sink.astype(jnp.float32)[None, :, :, None, None]
    cv = cv.astype(v.dtype)

    def block(i):
        start = i * A_BLOCK
        qi = lax.dynamic_slice_in_dim(q, start, A_BLOCK, axis=1)
        ki = lax.dynamic_slice_in_dim(kp, start, 3 * A_BLOCK, axis=1)
        vi = lax.dynamic_slice_in_dim(vp, start, 3 * A_BLOCK, axis=1)
        qpos = start + jnp.arange(A_BLOCK)
        kpos = start - A_BLOCK + jnp.arange(3 * A_BLOCK)
        ok = (kpos[None, :] >= 0) & (kpos[None, :] < L) & (jnp.abs(qpos[:, None] - kpos[None, :]) <= A_WINDOW)
        s_loc = jnp.einsum('bqkgd,bskd->bkgqs', qi, ki, preferred_element_type=jnp.float32) * ATTN_SCALE
        s_loc = jnp.where(ok, s_loc, NEG)
        s_ctx = jnp.einsum('bqkgd,bskd->bkgqs', qi, ck, preferred_element_type=jnp.float32) * ATTN_SCALE
        p = softmax_with_sink(jnp.concatenate([s_loc, s_ctx], axis=-1), sink_b)
        vv = jnp.concatenate([vi, cv], axis=1)
        return jnp.einsum('bkgqs,bskd->bqkgd', p.astype(v.dtype), vv)

    o = lax.map(block, jnp.arange(nb))
    return jnp.moveaxis(o, 0, 1).reshape(b, L, nkv, ng, dh)


def neighbourhood_attn_latent(q, k, v, ck, cv, rpb):
    b, L, nh, dh = q.shape
    rows = L // GRID_W
    kh = min(NH_ROWS, rows)
    ncb = GRID_W // NH_QCOLS
    qcol = np.arange(GRID_W).reshape(ncb, NH_QCOLS)
    cs0 = np.clip(qcol[:, 0] - NH_COLS // 2, 0, GRID_W - NH_KCOLS)
    kcol = cs0[:, None] + np.arange(NH_KCOLS)
    wstart = np.clip(qcol - NH_COLS // 2, 0, GRID_W - NH_COLS)
    col_ok = (kcol[:, None, :] >= wstart[:, :, None]) & (kcol[:, None, :] < wstart[:, :, None] + NH_COLS)
    dc_idx = np.clip(kcol[:, None, :] - qcol[:, :, None] + NH_COLS - 1, 0, 2 * NH_COLS - 2)
    rpb_c = rpb.astype(jnp.float32)[:, :, dc_idx]
    qg = q.reshape(b, rows, ncb, NH_QCOLS, nh, dh)
    kg = k.reshape(b, rows, GRID_W, nh, dh)
    vg = v.reshape(b, rows, GRID_W, nh, dh)
    cv = cv.astype(v.dtype)
    n_loc = kh * NH_KCOLS

    def row(r):
        rs = jnp.clip(r - kh // 2, 0, rows - kh)
        qr = lax.dynamic_index_in_dim(qg, r, axis=1, keepdims=False)
        kb = lax.dynamic_slice_in_dim(kg, rs, kh, axis=1)[:, :, kcol]
        vb = lax.dynamic_slice_in_dim(vg, rs, kh, axis=1)[:, :, kcol]
        s_loc = jnp.einsum('bcqhd,bicjhd->bhcqij', qr, kb, preferred_element_type=jnp.float32) * ATTN_SCALE
        dr_idx = rs + jnp.arange(kh) - r + NH_ROWS - 1
        bias = jnp.transpose(jnp.take(rpb_c, dr_idx, axis=1), (0, 2, 3, 1, 4))
        s_loc = jnp.where(col_ok[:, :, None, :], s_loc + bias, NEG).reshape(b, nh, ncb, NH_QCOLS, n_loc)
        s_ctx = jnp.einsum('bcqhd,bshd->bhcqs', qr, ck, preferred_element_type=jnp.float32) * ATTN_SCALE
        p = softmax_with_sink(jnp.concatenate([s_loc, s_ctx], axis=-1), None).astype(v.dtype)
        p_loc = p[..., :n_loc].reshape(b, nh, ncb, NH_QCOLS, kh, NH_KCOLS)
        o = jnp.einsum('bhcqij,bicjhd->bcqhd', p_loc, vb)
        return o + jnp.einsum('bhcqs,bshd->bcqhd', p[..., n_loc:], cv)

    o = lax.map(row, jnp.arange(rows))
    return jnp.moveaxis(o, 0, 1).reshape(b, L, nh, dh)


def short_conv(x, w):
    y = lax.conv_general_dilated(x, w[:, None, :].astype(x.dtype), window_strides=(1,),
                                 padding=[(CONV_K // 2, CONV_K // 2)],
                                 dimension_numbers=('NWC', 'WIO', 'NWC'),
                                 feature_group_count=x.shape[-1])
    return jax.nn.silu(y)


def gated_delta_chunked(q, k, v, g, beta, s0):
    b, L, h, dk = q.shape
    dv = v.shape[-1]
    C = DELTA_CHUNK
    n = L // C

    def chunks(t):
        return jnp.swapaxes(t.astype(jnp.float32).reshape((b, n, C) + t.shape[2:]), 2, 3)

    qc, kc, vc, bc = chunks(q), chunks(k), chunks(v), chunks(beta)
    gc = jnp.cumsum(chunks(g), axis=-1)
    tri_incl = np.tril(np.ones((C, C), dtype=bool))
    tri_strict = np.tril(np.ones((C, C), dtype=bool), k=-1)
    decay = jnp.exp(jnp.where(tri_incl, gc[..., :, None] - gc[..., None, :], NEG))
    a_mat = jnp.where(tri_strict, bc[..., :, None] * jnp.einsum('bnhtd,bnhjd->bnhtj', kc, kc) * decay, 0.0)
    rhs = jnp.concatenate([bc[..., None] * vc, (bc * jnp.exp(gc))[..., None] * kc], axis=-1)
    sol = lax.linalg.triangular_solve(a_mat + jnp.eye(C, dtype=jnp.float32), rhs,
                                      left_side=True, lower=True, unit_diagonal=True)
    w_v, w_k = sol[..., :dv], sol[..., dv:]
    p_qk = jnp.einsum('bnhtd,bnhjd->bnhtj', qc, kc) * decay
    q_dec = qc * jnp.exp(gc)[..., None]
    k_dec = kc * jnp.exp(gc[..., -1:] - gc)[..., None]
    chunk_decay = jnp.exp(gc[..., -1])

    def step(s, xs):
        wv, wk, pqk, qd, kd, cd = xs
        u = wv - jnp.einsum('bhtk,bhkv->bhtv', wk, s)
        o = jnp.einsum('bhtk,bhkv->bhtv', qd, s) + jnp.einsum('bhtj,bhjv->bhtv', pqk, u)
        s = cd[..., None, None] * s + jnp.einsum('bhtk,bhtv->bhkv', kd, u)
        return s, o

    xs = tuple(jnp.moveaxis(t, 1, 0) for t in (w_v, w_k, p_qk, q_dec, k_dec, chunk_decay))
    s_final, o = lax.scan(step, s0.astype(jnp.float32), xs)
    o = jnp.transpose(o, (1, 0, 3, 2, 4)).reshape(b, L, h, dv)
    return o, s_final


def hgrn2_chunked(q, k, v, log_f, s0):
    b, L, h, dk = q.shape
    dv = v.shape[-1]
    C = HGRN_CHUNK
    n = L // C

    def chunks(t):
        return jnp.swapaxes(t.astype(jnp.float32).reshape((b, n, C) + t.shape[2:]), 2, 3)

    qc, kc, vc = chunks(q), chunks(k), chunks(v)
    bcum = jnp.cumsum(chunks(log_f), axis=-2)
    tri_incl = np.tril(np.ones((C, C), dtype=bool))
    pair = jnp.exp(jnp.where(tri_incl[:, :, None], bcum[..., :, None, :] - bcum[..., None, :, :], NEG))
    att = jnp.einsum('bnhtd,bnhtjd,bnhjd->bnhtj', qc, pair, kc)
    o_intra = jnp.einsum('bnhtj,bnhjv->bnhtv', att, vc)
    q_dec = qc * jnp.exp(bcum)
    k_dec = kc * jnp.exp(bcum[..., -1:, :] - bcum)
    chunk_decay = jnp.exp(bcum[..., -1, :])

    def step(s, xs):
        kd, vv, cd = xs
        return cd[..., None] * s + jnp.einsum('bhtk,bhtv->bhkv', kd, vv), s

    xs = tuple(jnp.moveaxis(t, 1, 0) for t in (k_dec, vc, chunk_decay))
    s_final, s_start = lax.scan(step, s0.astype(jnp.float32), xs)
    s_start = jnp.moveaxis(s_start, 0, 1)
    o = o_intra + jnp.einsum('bnhtk,bnhkv->bnhtv', q_dec, s_start)
    o = jnp.transpose(o, (0, 1, 3, 2, 4)).reshape(b, L, h, dv)
    return o, s_final


def trunk_layer(x, mod, lp, ctx):
    b, L, _ = x.shape
    f32 = jnp.float32
    shift1, scale1, gate1, shift2, scale2, gate2 = jnp.split(mod[:, None, :], 6, axis=-1)
    h = rmsnorm(x, lp['norm_w'][0]) * (1.0 + scale1) + shift1
    splits = [int(o) for o in np.cumsum(IN_SPLITS)[:-1]]
    (a_q, a_k, a_v, b_q, b_k, b_v, b_z, b_a, b_b, c_q, c_f, c_i, c_g,
     d_q, d_k, d_v, g_logit) = jnp.split(h @ lp['w_in'], splits, axis=-1)

    a_q = a_q.reshape(b, L, A_HEADS, HEAD_DIM)
    a_k = a_k.reshape(b, L, A_KV_HEADS, HEAD_DIM)
    a_v = a_v.reshape(b, L, A_KV_HEADS, HEAD_DIM)
    sink = lp['attn_sink'].reshape(A_KV_HEADS, A_GROUP)
    if ctx is None:
        y_a = dense_attn(a_q.reshape(b, L, A_KV_HEADS, A_GROUP, HEAD_DIM), a_k, a_v, sink)
    else:
        y_a = window_attn_latent(rope_2d(a_q).reshape(b, L, A_KV_HEADS, A_GROUP, HEAD_DIM),
                                 rope_2d(a_k), a_v, ctx[0], ctx[1], sink)
    y_a = y_a.reshape(b, L, BRANCH_WIDTH)

    qkv = short_conv(jnp.concatenate([b_q, b_k, b_v], axis=-1), lp['delta_conv'])
    b_q, b_k, b_v = jnp.split(qkv, 3, axis=-1)
    b_q = l2norm(b_q.reshape(b, L, B_HEADS, HEAD_DIM)) * ATTN_SCALE
    b_k = l2norm(b_k.reshape(b, L, B_HEADS, HEAD_DIM))
    b_v = b_v.reshape(b, L, B_HEADS, HEAD_DIM)
    b_a = b_a.reshape(b, L, 2, B_HEADS).astype(f32)
    b_b = b_b.reshape(b, L, 2, B_HEADS).astype(f32)
    g = -jnp.exp(lp['delta_a_log'].astype(f32)) * jax.nn.softplus(b_a + lp['delta_dt_bias'].astype(f32))
    beta = jax.nn.sigmoid(b_b)
    if ctx is None:
        s0_f = jnp.zeros((b, B_HEADS, HEAD_DIM, HEAD_DIM), f32)
        s0_b = s0_f
    else:
        s0_f, s0_b = ctx[4][:, 0], ctx[4][:, 1]
    o_f, sd_f = gated_delta_chunked(b_q, b_k, b_v, g[:, :, 0], beta[:, :, 0], s0_f)
    o_b, sd_b = gated_delta_chunked(flip(b_q), flip(b_k), flip(b_v), flip(g[:, :, 1]), flip(beta[:, :, 1]), s0_b)
    y_b = rmsnorm(o_f + flip(o_b), lp['delta_norm_w']) * jax.nn.silu(b_z.reshape(b, L, B_HEADS, HEAD_DIM))
    y_b = y_b.reshape(b, L, BRANCH_WIDTH)

    lb = lp['hgrn_lb']
    c_f = c_f.reshape(b, L, 2, C_HEADS, HEAD_DIM).astype(f32)
    log_f = jnp.logaddexp(jnp.log(lb), jnp.log1p(-lb) + jax.nn.log_sigmoid(c_f))
    c_k = (1.0 - lb) * jax.nn.sigmoid(-c_f)
    c_q = jax.nn.silu(c_q).reshape(b, L, C_HEADS, HEAD_DIM)
    c_i = c_i.reshape(b, L, C_HEADS, HEAD_DIM)
    if ctx is None:
        h0_f = jnp.zeros((b, C_HEADS, HEAD_DIM, HEAD_DIM), f32)
        h0_b = h0_f
    else:
        h0_f, h0_b = ctx[5][:, 0], ctx[5][:, 1]
    o_f, sc_f = hgrn2_chunked(c_q, c_k[:, :, 0], c_i, log_f[:, :, 0], h0_f)
    o_b, sc_b = hgrn2_chunked(flip(c_q), flip(c_k[:, :, 1]), flip(c_i), flip(log_f[:, :, 1]), h0_b)
    y_c = rmsnorm((o_f + flip(o_b)) * jax.nn.sigmoid(c_g.reshape(b, L, C_HEADS, HEAD_DIM)), lp['hgrn_norm_w'])
    y_c = y_c.reshape(b, L, BRANCH_WIDTH)

    d_q = d_q.reshape(b, L, D_HEADS, HEAD_DIM)
    d_k = d_k.reshape(b, L, D_HEADS, HEAD_DIM)
    d_v = d_v.reshape(b, L, D_HEADS, HEAD_DIM)
    if ctx is None:
        y_d = dense_attn(d_q.reshape(b, L, D_HEADS, 1, HEAD_DIM), d_k, d_v, None)
    else:
        y_d = neighbourhood_attn_latent(d_q, d_k, d_v, ctx[2], ctx[3], lp['na_rpb'])
    y_d = y_d.reshape(b, L, BRANCH_WIDTH)

    ys = jnp.stack([y_a.astype(x.dtype), y_b.astype(x.dtype), y_c.astype(x.dtype), y_d.astype(x.dtype)], axis=2)
    y_proj = jnp.einsum('blkw,kwd->blkd', ys, lp['w_branch'])
    gates = jax.nn.sigmoid(g_logit.reshape(b, L, N_BRANCH, D_MODEL))
    merged = jnp.sum(gates * y_proj, axis=2)
    x = x + gate1 * (merged @ lp['w_out'])

    h2 = rmsnorm(x, lp['norm_w'][1]) * (1.0 + scale2) + shift2
    x = x + gate2 * (jnp.square(jax.nn.relu(h2 @ lp['mlp_w1'])) @ lp['mlp_w2'])

    if ctx is None:
        new_ctx = (a_k, a_v, d_k, d_v,
                   jnp.stack([sd_f, sd_b], axis=1).astype(x.dtype),
                   jnp.stack([sc_f, sc_b], axis=1).astype(x.dtype))
        return x, new_ctx
    return x, None


def setup_inputs(seed: int = 0) -> dict:
    key = jax.random.key(seed)
    ks = jax.random.split(key, 32)
    D = D_MODEL

    def nrm(k, shape, s):
        return jax.random.normal(k, shape, jnp.float32) * s

    dt = jnp.exp(jax.random.uniform(ks[16], (DEPTH, 2, B_HEADS), minval=math.log(1e-3), maxval=math.log(1e-1)))
    return {
        'x_prompt': nrm(ks[0], (BATCH, SEQ, D), 1.0),
        'x_sample': nrm(ks[1], (DEC_BATCH, DEC_SEQ, D), 1.0),
        'cache_attn_k': nrm(ks[2], (DEC_BATCH, DEPTH, PAST_LEN, A_KV_HEADS, HEAD_DIM), 1.0),
        'cache_attn_v': nrm(ks[3], (DEC_BATCH, DEPTH, PAST_LEN, A_KV_HEADS, HEAD_DIM), 1.0),
        'cache_na_k': nrm(ks[4], (DEC_BATCH, DEPTH, PAST_LEN, D_HEADS, HEAD_DIM), 1.0),
        'cache_na_v': nrm(ks[5], (DEC_BATCH, DEPTH, PAST_LEN, D_HEADS, HEAD_DIM), 1.0),
        'state_delta': nrm(ks[6], (DEC_BATCH, DEPTH, 2, B_HEADS, HEAD_DIM, HEAD_DIM), HEAD_DIM ** -0.5),
        'state_hgrn': nrm(ks[7], (DEC_BATCH, DEPTH, 2, C_HEADS, HEAD_DIM, HEAD_DIM), HEAD_DIM ** -0.5),
        'c': nrm(ks[8], (DEC_BATCH, D), 1.0),
        'c_ctx': nrm(ks[9], (D,), 1.0),
        'norm_w': 1.0 + nrm(ks[10], (DEPTH, 2, D), 0.02),
        'ada_w': nrm(ks[11], (DEPTH, D, 6 * D), 0.5 * D ** -0.5),
        'ada_b': nrm(ks[12], (DEPTH, 6 * D), 0.02),
        'w_in': nrm(ks[13], (DEPTH, D, N_IN), D ** -0.5),
        'attn_sink': nrm(ks[14], (DEPTH, A_HEADS), 1.0),
        'delta_conv': nrm(ks[15], (DEPTH, CONV_K, 3 * BRANCH_WIDTH), CONV_K ** -0.5),
        'delta_a_log': jnp.log(jax.random.uniform(ks[17], (DEPTH, 2, B_HEADS), minval=1.0, maxval=16.0)),
        'delta_dt_bias': dt + jnp.log(-jnp.expm1(-dt)),
        'delta_norm_w': 1.0 + nrm(ks[18], (DEPTH, HEAD_DIM), 0.02),
        'hgrn_lb': nrm(ks[19], (DEPTH, 2, C_HEADS * HEAD_DIM), 1.0),
        'hgrn_norm_w': 1.0 + nrm(ks[20], (DEPTH, HEAD_DIM), 0.02),
        'na_rpb': nrm(ks[21], (DEPTH, D_HEADS, 2 * NH_ROWS - 1, 2 * NH_COLS - 1), 0.1),
        'w_branch': nrm(ks[22], (DEPTH, N_BRANCH, BRANCH_WIDTH, D), BRANCH_WIDTH ** -0.5),
        'w_out': nrm(ks[23], (DEPTH, D, D), D ** -0.5),
        'mlp_w1': nrm(ks[24], (DEPTH, D, D_FF), D ** -0.5),
        'mlp_w2': nrm(ks[25], (DEPTH, D_FF, D), D_FF ** -0.5),
        'final_norm_w': 1.0 + nrm(ks[26], (D,), 0.02),
    }


def reference(x_prompt, x_sample, cache_attn_k, cache_attn_v, cache_na_k, cache_na_v, state_delta, state_hgrn,
              c, c_ctx, norm_w, ada_w, ada_b, w_in, attn_sink, delta_conv, delta_a_log, delta_dt_bias,
              delta_norm_w, hgrn_lb, hgrn_norm_w, na_rpb, w_branch, w_out, mlp_w1, mlp_w2, final_norm_w):
    lb = jnp.cumsum(jax.nn.softmax(hgrn_lb.astype(jnp.float32), axis=0), axis=0)
    lb = lb - lb[:1]
    xp, xs = x_prompt, x_sample
    ak_l, av_l, nk_l, nv_l, sd_l, sh_l = [], [], [], [], [], []
    for l in range(DEPTH):
        lp = {
            'norm_w': norm_w[l], 'w_in': w_in[l], 'attn_sink': attn_sink[l],
            'delta_conv': delta_conv[l], 'delta_a_log': delta_a_log[l], 'delta_dt_bias': delta_dt_bias[l],
            'delta_norm_w': delta_norm_w[l], 'hgrn_lb': lb[l].reshape(2, C_HEADS, HEAD_DIM),
            'hgrn_norm_w': hgrn_norm_w[l], 'na_rpb': na_rpb[l], 'w_branch': w_branch[l],
            'w_out': w_out[l], 'mlp_w1': mlp_w1[l], 'mlp_w2': mlp_w2[l],
        }
        mod_ctx = adaln(c_ctx[None, :], ada_w[l], ada_b[l])
        mod_lat = adaln(c, ada_w[l], ada_b[l])
        xp, (ak, av, nk, nv, sd, sh) = trunk_layer(xp, mod_ctx, lp, None)
        ak_l.append(ak)
        av_l.append(av)
        nk_l.append(nk)
        nv_l.append(nv)
        sd_l.append(sd)
        sh_l.append(sh)
        ctx = (cache_attn_k[:, l], cache_attn_v[:, l], cache_na_k[:, l], cache_na_v[:, l],
               state_delta[:, l], state_hgrn[:, l])
        xs, _ = trunk_layer(xs, mod_lat, lp, ctx)
    y_prompt = rmsnorm(xp, final_norm_w)
    y_sample = rmsnorm(xs, final_norm_w)
    new_attn_k = jnp.stack(ak_l, axis=1)
    new_attn_v = jnp.stack(av_l, axis=1)
    new_na_k = jnp.stack(nk_l, axis=1)
    new_na_v = jnp.stack(nv_l, axis=1)
    new_state_delta = jnp.stack(sd_l, axis=1)
    new_state_hgrn = jnp.stack(sh_l, axis=1)
    return (y_prompt, y_sample, new_attn_k, new_attn_v, new_na_k, new_na_v, new_state_delta, new_state_hgrn)
```

```python
import functools
import math

import jax
import jax.numpy as jnp
import numpy as np
from jax import lax
from jax.experimental import pallas as pl
from jax.experimental.pallas import tpu as pltpu

F32 = jnp.float32
BF16 = jnp.bfloat16

D_MODEL = 1024
BATCH = 16
SEQ = 256
DEPTH = 4
DEC_BATCH = 8
DEC_SEQ = 1024
PAST_LEN = 512
GRID_W = 64
HEAD_DIM = 64
BRANCH_WIDTH = 512
N_BRANCH = 4
A_HEADS = 8
A_KV_HEADS = 2
A_GROUP = 4
A_WINDOW = 128
A_BLOCK = 128
Q_BLOCK = 128
B_HEADS = 8
DELTA_CHUNK = 64
CONV_K = 5
C_HEADS = 8
HGRN_CHUNK = 16
D_HEADS = 8
NH_ROWS = 8
NH_COLS = 16
NH_QCOLS = 16
NH_KCOLS = 32
D_FF = 4 * D_MODEL
ROPE_BASE = 10000.0
ATTN_SCALE = HEAD_DIM ** -0.5
EPS = 1e-6
NEG = -1e30

N_CTX_TOK = BATCH * SEQ
N_LAT_TOK = DEC_BATCH * DEC_SEQ
N_TOK = N_CTX_TOK + N_LAT_TOK
N_MOD_ROWS = 1 + DEC_BATCH

_REF_COLS = (("a_q", 512), ("a_k", 128), ("a_v", 128), ("b_q", 512), ("b_k", 512), ("b_v", 512),
             ("b_z", 512), ("b_ab", 32), ("c_q", 512), ("c_f", 1024), ("c_i", 512), ("c_g", 512),
             ("d_q", 512), ("d_k", 512), ("d_v", 512), ("g", 4096))
_P_ORDER = ("g", "a_q", "b_q", "b_k", "b_v", "b_z", "c_q", "c_f", "c_i", "c_g", "d_q", "d_k", "d_v",
            "a_k", "a_v", "b_ab")
LANE = 128
N_P = 11264


def _layout():
    ref_off, o = {}, 0
    for name, w in _REF_COLS:
        ref_off[name] = (o, w)
        o += w
    p_off, o = {}, 0
    for name in _P_ORDER:
        w = ref_off[name][1]
        p_off[name] = (o, w)
        o += -(-w // LANE) * LANE
    assert o <= N_P
    return ref_off, p_off


REF_OFF, P_OFF = _layout()

VMEM_LIMIT = 56 * 1024 * 1024


def _mod_row(i, tm):
    nct = N_CTX_TOK // tm
    tpl = DEC_SEQ // tm
    return jnp.where(i < nct, 0, 1 + (i - nct) // tpl)


def _adaln_kernel(c_ref, w_ref, b_ref, o_ref):
    c = c_ref[...]
    s = c * jax.nn.sigmoid(c)
    o_ref[0] = jnp.dot(s, w_ref[0], preferred_element_type=F32) + b_ref[0]


def _adaln(cvec, ada_w, ada_b):
    tn = 1536
    n6 = 6 * D_MODEL
    rows = cvec.shape[0]
    return pl.pallas_call(
        _adaln_kernel,
        grid=(DEPTH, n6 // tn),
        in_specs=[pl.BlockSpec((rows, D_MODEL), lambda l, j: (0, 0)),
                  pl.BlockSpec((1, D_MODEL, tn), lambda l, j: (l, 0, j)),
                  pl.BlockSpec((1, 1, tn), lambda l, j: (l, 0, j))],
        out_specs=pl.BlockSpec((1, rows, tn), lambda l, j: (l, 0, j)),
        out_shape=jax.ShapeDtypeStruct((DEPTH, rows, n6), F32),
        compiler_params=pltpu.CompilerParams(dimension_semantics=("arbitrary", "arbitrary"),
                                             vmem_limit_bytes=VMEM_LIMIT),
        name="adaln",
    )(cvec, ada_w, ada_b.reshape(DEPTH, 1, n6))


ROW_CHUNK = 128


def _norm_mod_to(h_ref, x_ref, nw_ref, shift, scale):
    n = x_ref.shape[0] // ROW_CHUNK

    def body(r, carry):
        rows = pl.ds(pl.multiple_of(r * ROW_CHUNK, ROW_CHUNK), ROW_CHUNK)
        x = x_ref[rows, :]
        y = x * lax.rsqrt(jnp.mean(x * x, axis=-1, keepdims=True) + EPS) * nw_ref[...]
        h_ref[rows, :] = (y * (1.0 + scale) + shift).astype(BF16)
        return carry

    lax.fori_loop(0, n, body, 0)


def _inproj_kernel(x_ref, mod_ref, nw_ref, w_ref, o_ref, h_ref):
    @pl.when(pl.program_id(1) == 0)
    def _():
        _norm_mod_to(h_ref, x_ref, nw_ref, mod_ref[0, 0:1, :], mod_ref[0, 1:2, :])

    o_ref[...] = jnp.dot(h_ref[...], w_ref[...], preferred_element_type=F32)


def _inproj(x, mod, nw, w):
    tm, tn = 1024, 1024
    return pl.pallas_call(
        _inproj_kernel,
        grid=(N_TOK // tm, N_P // tn),
        in_specs=[pl.BlockSpec((tm, D_MODEL), lambda i, j: (i, 0)),
                  pl.BlockSpec((1, 6, D_MODEL), lambda i, j: (_mod_row(i, tm), 0, 0)),
                  pl.BlockSpec((1, D_MODEL), lambda i, j: (0, 0)),
                  pl.BlockSpec((D_MODEL, tn), lambda i, j: (0, j))],
        out_specs=pl.BlockSpec((tm, tn), lambda i, j: (i, j)),
        out_shape=jax.ShapeDtypeStruct((N_TOK, N_P), F32),
        scratch_shapes=[pltpu.VMEM((tm, D_MODEL), BF16)],
        compiler_params=pltpu.CompilerParams(dimension_semantics=("arbitrary", "arbitrary"),
                                             vmem_limit_bytes=VMEM_LIMIT),
        name="inproj",
    )(x, mod, nw, w)


def _merge_kernel(x_ref, ya_ref, yb_ref, yc_ref, yd_ref, g_ref, mod_ref, wb_ref, wo_ref, o_ref):
    merged = None
    for k, y_ref in enumerate((ya_ref, yb_ref, yc_ref, yd_ref)):
        yp = jnp.dot(y_ref[...].astype(BF16), wb_ref[k], preferred_element_type=F32)
        t = jax.nn.sigmoid(g_ref[:, k * D_MODEL:(k + 1) * D_MODEL]) * yp
        merged = t if merged is None else merged + t
    o = jnp.dot(merged.astype(BF16), wo_ref[...], preferred_element_type=F32)
    o_ref[...] = x_ref[...] + mod_ref[0, 2:3, :] * o


def _merge(x, ya, yb, yc, yd, p, mod, wb, wo):
    tm = 256
    yspec = pl.BlockSpec((tm, BRANCH_WIDTH), lambda i: (i, 0))
    return pl.pallas_call(
        _merge_kernel,
        grid=(N_TOK // tm,),
        in_specs=[pl.BlockSpec((tm, D_MODEL), lambda i: (i, 0)),
                  yspec, yspec, yspec, yspec,
                  pl.BlockSpec((tm, N_BRANCH * D_MODEL), lambda i: (i, 0)),
                  pl.BlockSpec((1, 6, D_MODEL), lambda i: (_mod_row(i, tm), 0, 0)),
                  pl.BlockSpec((N_BRANCH, BRANCH_WIDTH, D_MODEL), lambda i: (0, 0, 0)),
                  pl.BlockSpec((D_MODEL, D_MODEL), lambda i: (0, 0))],
        out_specs=pl.BlockSpec((tm, D_MODEL), lambda i: (i, 0)),
        out_shape=jax.ShapeDtypeStruct((N_TOK, D_MODEL), F32),
        compiler_params=pltpu.CompilerParams(dimension_semantics=("arbitrary",),
                                             vmem_limit_bytes=VMEM_LIMIT),
        name="merge",
    )(x, ya, yb, yc, yd, p, mod, wb, wo)


def _mlp_kernel(x_ref, mod_ref, nw_ref, w1_ref, w2_ref, fw_ref, o_ref, h_ref, acc_ref, *, final):
    f = pl.program_id(1)

    @pl.when(f == 0)
    def _():
        _norm_mod_to(h_ref, x_ref, nw_ref, mod_ref[0, 3:4, :], mod_ref[0, 4:5, :])

    a = jnp.dot(h_ref[...], w1_ref[...], preferred_element_type=F32)
    a = jnp.square(jnp.maximum(a, 0.0)).astype(BF16)
    contrib = jnp.dot(a, w2_ref[...], preferred_element_type=F32)

    @pl.when(f == 0)
    def _():
        acc_ref[...] = contrib

    @pl.when(f != 0)
    def _():
        acc_ref[...] += contrib

    @pl.when(f == pl.num_programs(1) - 1)
    def _():
        y = x_ref[...] + mod_ref[0, 5:6, :] * acc_ref[...]
        if final:
            y = y * lax.rsqrt(jnp.mean(y * y, axis=-1, keepdims=True) + EPS) * fw_ref[...]
        o_ref[...] = y


def _mlp(x, mod, nw, w1, w2, fw, final):
    tm, tf = 1024, 512
    return pl.pallas_call(
        functools.partial(_mlp_kernel, final=final),
        grid=(N_TOK // tm, D_FF // tf),
        in_specs=[pl.BlockSpec((tm, D_MODEL), lambda i, f: (i, 0)),
                  pl.BlockSpec((1, 6, D_MODEL), lambda i, f: (_mod_row(i, tm), 0, 0)),
                  pl.BlockSpec((1, D_MODEL), lambda i, f: (0, 0)),
                  pl.BlockSpec((D_MODEL, tf), lambda i, f: (0, f)),
                  pl.BlockSpec((tf, D_MODEL), lambda i, f: (f, 0)),
                  pl.BlockSpec((1, D_MODEL), lambda i, f: (0, 0))],
        out_specs=pl.BlockSpec((tm, D_MODEL), lambda i, f: (i, 0)),
        out_shape=jax.ShapeDtypeStruct((N_TOK, D_MODEL), F32),
        scratch_shapes=[pltpu.VMEM((tm, D_MODEL), BF16), pltpu.VMEM((tm, D_MODEL), F32)],
        compiler_params=pltpu.CompilerParams(dimension_semantics=("arbitrary", "arbitrary"),
                                             vmem_limit_bytes=VMEM_LIMIT),
        name="mlp",
    )(x, mod, nw, w1, w2, fw)


def rmsnorm(x, w):
    xf = x.astype(jnp.float32)
    y = xf * lax.rsqrt(jnp.mean(xf * xf, axis=-1, keepdims=True) + EPS)
    return (y * w.astype(jnp.float32)).astype(x.dtype)


def l2norm(x):
    xf = x.astype(jnp.float32)
    return xf * lax.rsqrt(jnp.sum(xf * xf, axis=-1, keepdims=True) + EPS)


def flip(t):
    return jnp.flip(t, axis=1)


def rope_2d(x):
    L = x.shape[1]
    t = jnp.arange(L)
    half = HEAD_DIM // 2
    quarter = half // 2
    inv = ROPE_BASE ** (-jnp.arange(quarter, dtype=jnp.float32) / quarter)

    def rot(xa, pos):
        ang = pos.astype(jnp.float32)[:, None] * inv[None, :]
        cos = jnp.cos(ang)[None, :, None, :].astype(x.dtype)
        sin = jnp.sin(ang)[None, :, None, :].astype(x.dtype)
        x1, x2 = xa[..., :quarter], xa[..., quarter:]
        return jnp.concatenate([x1 * cos - x2 * sin, x2 * cos + x1 * sin], axis=-1)

    return jnp.concatenate([rot(x[..., :half], t // GRID_W), rot(x[..., half:], t % GRID_W)], axis=-1)


def softmax_with_sink(s, sink):
    m = jnp.max(s, axis=-1, keepdims=True)
    if sink is not None:
        m = jnp.maximum(m, sink)
    p = jnp.exp(s - m)
    den = jnp.sum(p, axis=-1, keepdims=True)
    if sink is not None:
        den = den + jnp.exp(sink - m)
    return p / den


def dense_attn(q, k, v, sink):
    b, lq, nkv, ng, dh = q.shape
    nb = lq // Q_BLOCK
    qb = jnp.moveaxis(q.reshape(b, nb, Q_BLOCK, nkv, ng, dh), 1, 0)
    sink_b = None if sink is None else sink.astype(jnp.float32)[None, :, :, None, None]

    def block(qi):
        s = jnp.einsum('bqkgd,bskd->bkgqs', qi, k, preferred_element_type=jnp.float32) * ATTN_SCALE
        p = softmax_with_sink(s, sink_b)
        return jnp.einsum('bkgqs,bskd->bqkgd', p.astype(v.dtype), v)

    o = lax.map(block, qb)
    return jnp.moveaxis(o, 0, 1).reshape(b, lq, nkv, ng, dh)


def window_attn_latent(q, k, v, ck, cv, sink):
    b, L, nkv, ng, dh = q.shape
    nb = L // A_BLOCK
    pad = ((0, 0), (A_BLOCK, A_BLOCK), (0, 0), (0, 0))
    kp = jnp.pad(k, pad)
    vp = jnp.pad(v, pad)
    sink_b = sink.astype(jnp.float32)[None, :, :, None, None]
    cv = cv.astype(v.dtype)

    def block(i):
        start = i * A_BLOCK
        qi = lax.dynamic_slice_in_dim(q, start, A_BLOCK, axis=1)
        ki = lax.dynamic_slice_in_dim(kp, start, 3 * A_BLOCK, axis=1)
        vi = lax.dynamic_slice_in_dim(vp, start, 3 * A_BLOCK, axis=1)
        qpos = start + jnp.arange(A_BLOCK)
        kpos = start - A_BLOCK + jnp.arange(3 * A_BLOCK)
        ok = (kpos[None, :] >= 0) & (kpos[None, :] < L) & (jnp.abs(qpos[:, None] - kpos[None, :]) <= A_WINDOW)
        s_loc = jnp.einsum('bqkgd,bskd->bkgqs', qi, ki, preferred_element_type=jnp.float32) * ATTN_SCALE
        s_loc = jnp.where(ok, s_loc, NEG)
        s_ctx = jnp.einsum('bqkgd,bskd->bkgqs', qi, ck, preferred_element_type=jnp.float32) * ATTN_SCALE
        p = softmax_with_sink(jnp.concatenate([s_loc, s_ctx], axis=-1), sink_b)
        vv = jnp.concatenate([vi, cv], axis=1)
        return jnp.einsum('bkgqs,bskd->bqkgd', p.astype(v.dtype), vv)

    o = lax.map(block, jnp.arange(nb))
    return jnp.moveaxis(o, 0, 1).reshape(b, L, nkv, ng, dh)


def neighbourhood_attn_latent(q, k, v, ck, cv, rpb):
    b, L, nh, dh = q.shape
    rows = L // GRID_W
    kh = min(NH_ROWS, rows)
    ncb = GRID_W // NH_QCOLS
    qcol = np.arange(GRID_W).reshape(ncb, NH_QCOLS)
    cs0 = np.clip(qcol[:, 0] - NH_COLS // 2, 0, GRID_W - NH_KCOLS)
    kcol = cs0[:, None] + np.arange(NH_KCOLS)
    wstart = np.clip(qcol - NH_COLS // 2, 0, GRID_W - NH_COLS)
    col_ok = (kcol[:, None, :] >= wstart[:, :, None]) & (kcol[:, None, :] < wstart[:, :, None] + NH_COLS)
    dc_idx = np.clip(kcol[:, None, :] - qcol[:, :, None] + NH_COLS - 1, 0, 2 * NH_COLS - 2)
    rpb_c = rpb.astype(jnp.float32)[:, :, dc_idx]
    qg = q.reshape(b, rows, ncb, NH_QCOLS, nh, dh)
    kg = k.reshape(b, rows, GRID_W, nh, dh)
    vg = v.reshape(b, rows, GRID_W, nh, dh)
    cv = cv.astype(v.dtype)
    n_loc = kh * NH_KCOLS

    def row(r):
        rs = jnp.clip(r - kh // 2, 0, rows - kh)
        qr = lax.dynamic_index_in_dim(qg, r, axis=1, keepdims=False)
        kb = lax.dynamic_slice_in_dim(kg, rs, kh, axis=1)[:, :, kcol]
        vb = lax.dynamic_slice_in_dim(vg, rs, kh, axis=1)[:, :, kcol]
        s_loc = jnp.einsum('bcqhd,bicjhd->bhcqij', qr, kb, preferred_element_type=jnp.float32) * ATTN_SCALE
        dr_idx = rs + jnp.arange(kh) - r + NH_ROWS - 1
        bias = jnp.transpose(jnp.take(rpb_c, dr_idx, axis=1), (0, 2, 3, 1, 4))
        s_loc = jnp.where(col_ok[:, :, None, :], s_loc + bias, NEG).reshape(b, nh, ncb, NH_QCOLS, n_loc)
        s_ctx = jnp.einsum('bcqhd,bshd->bhcqs', qr, ck, preferred_element_type=jnp.float32) * ATTN_SCALE
        p = softmax_with_sink(jnp.concatenate([s_loc, s_ctx], axis=-1), None).astype(v.dtype)
        p_loc = p[..., :n_loc].reshape(b, nh, ncb, NH_QCOLS, kh, NH_KCOLS)
        o = jnp.einsum('bhcqij,bicjhd->bcqhd', p_loc, vb)
        return o + jnp.einsum('bhcqs,bshd->bcqhd', p[..., n_loc:], cv)

    o = lax.map(row, jnp.arange(rows))
    return jnp.moveaxis(o, 0, 1).reshape(b, L, nh, dh)


def short_conv(x, w):
    y = lax.conv_general_dilated(x, w[:, None, :].astype(x.dtype), window_strides=(1,),
                                 padding=[(CONV_K // 2, CONV_K // 2)],
                                 dimension_numbers=('NWC', 'WIO', 'NWC'),
                                 feature_group_count=x.shape[-1])
    return jax.nn.silu(y)


def gated_delta_chunked(q, k, v, g, beta, s0):
    b, L, h, dk = q.shape
    dv = v.shape[-1]
    C = DELTA_CHUNK
    n = L // C

    def chunks(t):
        return jnp.swapaxes(t.astype(jnp.float32).reshape((b, n, C) + t.shape[2:]), 2, 3)

    qc, kc, vc, bc = chunks(q), chunks(k), chunks(v), chunks(beta)
    gc = jnp.cumsum(chunks(g), axis=-1)
    tri_incl = np.tril(np.ones((C, C), dtype=bool))
    tri_strict = np.tril(np.ones((C, C), dtype=bool), k=-1)
    decay = jnp.exp(jnp.where(tri_incl, gc[..., :, None] - gc[..., None, :], NEG))
    a_mat = jnp.where(tri_strict, bc[..., :, None] * jnp.einsum('bnhtd,bnhjd->bnhtj', kc, kc) * decay, 0.0)
    rhs = jnp.concatenate([bc[..., None] * vc, (bc * jnp.exp(gc))[..., None] * kc], axis=-1)
    sol = lax.linalg.triangular_solve(a_mat + jnp.eye(C, dtype=jnp.float32), rhs,
                                      left_side=True, lower=True, unit_diagonal=True)
    w_v, w_k = sol[..., :dv], sol[..., dv:]
    p_qk = jnp.einsum('bnhtd,bnhjd->bnhtj', qc, kc) * decay
    q_dec = qc * jnp.exp(gc)[..., None]
    k_dec = kc * jnp.exp(gc[..., -1:] - gc)[..., None]
    chunk_decay = jnp.exp(gc[..., -1])

    def step(s, xs):
        wv, wk, pqk, qd, kd, cd = xs
        u = wv - jnp.einsum('bhtk,bhkv->bhtv', wk, s)
        o = jnp.einsum('bhtk,bhkv->bhtv', qd, s) + jnp.einsum('bhtj,bhjv->bhtv', pqk, u)
        s = cd[..., None, None] * s + jnp.einsum('bhtk,bhtv->bhkv', kd, u)
        return s, o

    xs = tuple(jnp.moveaxis(t, 1, 0) for t in (w_v, w_k, p_qk, q_dec, k_dec, chunk_decay))
    s_final, o = lax.scan(step, s0.astype(jnp.float32), xs)
    o = jnp.transpose(o, (1, 0, 3, 2, 4)).reshape(b, L, h, dv)
    return o, s_final


def hgrn2_chunked(q, k, v, log_f, s0):
    b, L, h, dk = q.shape
    dv = v.shape[-1]
    C = HGRN_CHUNK
    n = L // C

    def chunks(t):
        return jnp.swapaxes(t.astype(jnp.float32).reshape((b, n, C) + t.shape[2:]), 2, 3)

    qc, kc, vc = chunks(q), chunks(k), chunks(v)
    bcum = jnp.cumsum(chunks(log_f), axis=-2)
    tri_incl = np.tril(np.ones((C, C), dtype=bool))
    pair = jnp.exp(jnp.where(tri_incl[:, :, None], bcum[..., :, None, :] - bcum[..., None, :, :], NEG))
    att = jnp.einsum('bnhtd,bnhtjd,bnhjd->bnhtj', qc, pair, kc)
    o_intra = jnp.einsum('bnhtj,bnhjv->bnhtv', att, vc)
    q_dec = qc * jnp.exp(bcum)
    k_dec = kc * jnp.exp(bcum[..., -1:, :] - bcum)
    chunk_decay = jnp.exp(bcum[..., -1, :])

    def step(s, xs):
        kd, vv, cd = xs
        return cd[..., None] * s + jnp.einsum('bhtk,bhtv->bhkv', kd, vv), s

    xs = tuple(jnp.moveaxis(t, 1, 0) for t in (k_dec, vc, chunk_decay))
    s_final, s_start = lax.scan(step, s0.astype(jnp.float32), xs)
    s_start = jnp.moveaxis(s_start, 0, 1)
    o = o_intra + jnp.einsum('bnhtk,bnhkv->bnhtv', q_dec, s_start)
    o = jnp.transpose(o, (0, 1, 3, 2, 4)).reshape(b, L, h, dv)
    return o, s_final


def _pcol(p, name):
    o, w = P_OFF[name]
    return p[..., o:o + w]


def _mixers(p, lp, ctx):
    b, L, _ = p.shape
    f32 = jnp.float32
    a_q = _pcol(p, "a_q").reshape(b, L, A_HEADS, HEAD_DIM)
    a_k = _pcol(p, "a_k").reshape(b, L, A_KV_HEADS, HEAD_DIM)
    a_v = _pcol(p, "a_v").reshape(b, L, A_KV_HEADS, HEAD_DIM)
    sink = lp['attn_sink'].reshape(A_KV_HEADS, A_GROUP)
    if ctx is None:
        y_a = dense_attn(a_q.reshape(b, L, A_KV_HEADS, A_GROUP, HEAD_DIM), a_k, a_v, sink)
    else:
        y_a = window_attn_latent(rope_2d(a_q).reshape(b, L, A_KV_HEADS, A_GROUP, HEAD_DIM),
                                 rope_2d(a_k), a_v, ctx[0], ctx[1], sink)
    y_a = y_a.reshape(b, L, BRANCH_WIDTH)

    qkv = short_conv(jnp.concatenate([_pcol(p, "b_q"), _pcol(p, "b_k"), _pcol(p, "b_v")], axis=-1),
                     lp['delta_conv'])
    b_q, b_k, b_v = jnp.split(qkv, 3, axis=-1)
    b_q = l2norm(b_q.reshape(b, L, B_HEADS, HEAD_DIM)) * ATTN_SCALE
    b_k = l2norm(b_k.reshape(b, L, B_HEADS, HEAD_DIM))
    b_v = b_v.reshape(b, L, B_HEADS, HEAD_DIM)
    b_ab = _pcol(p, "b_ab")
    b_a = b_ab[..., :16].reshape(b, L, 2, B_HEADS).astype(f32)
    b_b = b_ab[..., 16:].reshape(b, L, 2, B_HEADS).astype(f32)
    g = -jnp.exp(lp['delta_a_log'].astype(f32)) * jax.nn.softplus(b_a + lp['delta_dt_bias'].astype(f32))
    beta = jax.nn.sigmoid(b_b)
    if ctx is None:
        s0_f = jnp.zeros((b, B_HEADS, HEAD_DIM, HEAD_DIM), f32)
        s0_b = s0_f
    else:
        s0_f, s0_b = ctx[4][:, 0], ctx[4][:, 1]
    o_f, sd_f = gated_delta_chunked(b_q, b_k, b_v, g[:, :, 0], beta[:, :, 0], s0_f)
    o_b, sd_b = gated_delta_chunked(flip(b_q), flip(b_k), flip(b_v), flip(g[:, :, 1]), flip(beta[:, :, 1]), s0_b)
    y_b = rmsnorm(o_f + flip(o_b), lp['delta_norm_w']) * jax.nn.silu(_pcol(p, "b_z").reshape(b, L, B_HEADS, HEAD_DIM))
    y_b = y_b.reshape(b, L, BRANCH_WIDTH)

    lb = lp['hgrn_lb']
    c_f = _pcol(p, "c_f").reshape(b, L, 2, C_HEADS, HEAD_DIM).astype(f32)
    log_f = jnp.logaddexp(jnp.log(lb), jnp.log1p(-lb) + jax.nn.log_sigmoid(c_f))
    c_k = (1.0 - lb) * jax.nn.sigmoid(-c_f)
    c_q = jax.nn.silu(_pcol(p, "c_q")).reshape(b, L, C_HEADS, HEAD_DIM)
    c_i = _pcol(p, "c_i").reshape(b, L, C_HEADS, HEAD_DIM)
    if ctx is None:
        h0_f = jnp.zeros((b, C_HEADS, HEAD_DIM, HEAD_DIM), f32)
        h0_b = h0_f
    else:
        h0_f, h0_b = ctx[5][:, 0], ctx[5][:, 1]
    o_f, sc_f = hgrn2_chunked(c_q, c_k[:, :, 0], c_i, log_f[:, :, 0], h0_f)
    o_b, sc_b = hgrn2_chunked(flip(c_q), flip(c_k[:, :, 1]), flip(c_i), flip(log_f[:, :, 1]), h0_b)
    y_c = rmsnorm((o_f + flip(o_b)) * jax.nn.sigmoid(_pcol(p, "c_g").reshape(b, L, C_HEADS, HEAD_DIM)),
                  lp['hgrn_norm_w'])
    y_c = y_c.reshape(b, L, BRANCH_WIDTH)

    d_q = _pcol(p, "d_q").reshape(b, L, D_HEADS, HEAD_DIM)
    d_k = _pcol(p, "d_k").reshape(b, L, D_HEADS, HEAD_DIM)
    d_v = _pcol(p, "d_v").reshape(b, L, D_HEADS, HEAD_DIM)
    if ctx is None:
        y_d = dense_attn(d_q.reshape(b, L, D_HEADS, 1, HEAD_DIM), d_k, d_v, None)
    else:
        y_d = neighbourhood_attn_latent(d_q, d_k, d_v, ctx[2], ctx[3], lp['na_rpb'])
    y_d = y_d.reshape(b, L, BRANCH_WIDTH)

    new_ctx = None
    if ctx is None:
        new_ctx = (a_k, a_v, d_k, d_v, jnp.stack([sd_f, sd_b], axis=1), jnp.stack([sc_f, sc_b], axis=1))
    return (y_a, y_b, y_c, y_d), new_ctx


def _pack_w_in(w_in):
    parts = []
    used = 0
    for name in _P_ORDER:
        o, w = REF_OFF[name]
        parts.append(w_in[..., o:o + w])
        pw = -(-w // LANE) * LANE
        if pw != w:
            parts.append(jnp.zeros(w_in.shape[:-1] + (pw - w,), w_in.dtype))
        used += pw
    parts.append(jnp.zeros(w_in.shape[:-1] + (N_P - used,), w_in.dtype))
    return jnp.concatenate(parts, axis=-1).astype(BF16)


def kernel(x_prompt, x_sample, cache_attn_k, cache_attn_v, cache_na_k, cache_na_v, state_delta, state_hgrn,
           c, c_ctx, norm_w, ada_w, ada_b, w_in, attn_sink, delta_conv, delta_a_log, delta_dt_bias,
           delta_norm_w, hgrn_lb, hgrn_norm_w, na_rpb, w_branch, w_out, mlp_w1, mlp_w2, final_norm_w):
    lb = jnp.cumsum(jax.nn.softmax(hgrn_lb.astype(F32), axis=0), axis=0)
    lb = lb - lb[:1]

    cvec = jnp.concatenate([c_ctx[None, :], c, jnp.zeros((16 - N_MOD_ROWS, D_MODEL), F32)], axis=0)
    mod = _adaln(cvec, ada_w, ada_b).reshape(DEPTH, 16, 6, D_MODEL)

    w_in_p = _pack_w_in(w_in)
    wb = w_branch.astype(BF16)
    wo = w_out.astype(BF16)
    w1 = mlp_w1.astype(BF16)
    w2 = mlp_w2.astype(BF16)
    fw = final_norm_w.reshape(1, D_MODEL)

    x = jnp.concatenate([x_prompt.reshape(N_CTX_TOK, D_MODEL), x_sample.reshape(N_LAT_TOK, D_MODEL)], axis=0)
    ak_l, av_l, nk_l, nv_l, sd_l, sh_l = [], [], [], [], [], []
    for l in range(DEPTH):
        lp = {
            'attn_sink': attn_sink[l], 'delta_conv': delta_conv[l], 'delta_a_log': delta_a_log[l],
            'delta_dt_bias': delta_dt_bias[l], 'delta_norm_w': delta_norm_w[l],
            'hgrn_lb': lb[l].reshape(2, C_HEADS, HEAD_DIM), 'hgrn_norm_w': hgrn_norm_w[l],
            'na_rpb': na_rpb[l],
        }
        p = _inproj(x, mod[l], norm_w[l, 0].reshape(1, D_MODEL), w_in_p[l])
        ys_c, (ak, av, nk, nv, sd, sh) = _mixers(p[:N_CTX_TOK].reshape(BATCH, SEQ, N_P), lp, None)
        ctx = (cache_attn_k[:, l], cache_attn_v[:, l], cache_na_k[:, l], cache_na_v[:, l],
               state_delta[:, l], state_hgrn[:, l])
        ys_l, _ = _mixers(p[N_CTX_TOK:].reshape(DEC_BATCH, DEC_SEQ, N_P), lp, ctx)
        ys = [jnp.concatenate([yc.reshape(N_CTX_TOK, BRANCH_WIDTH), yl.reshape(N_LAT_TOK, BRANCH_WIDTH)], axis=0)
              for yc, yl in zip(ys_c, ys_l)]
        ak_l.append(ak)
        av_l.append(av)
        nk_l.append(nk)
        nv_l.append(nv)
        sd_l.append(sd)
        sh_l.append(sh)
        x = _merge(x, ys[0], ys[1], ys[2], ys[3], p, mod[l], wb[l], wo[l])
        x = _mlp(x, mod[l], norm_w[l, 1].reshape(1, D_MODEL), w1[l], w2[l], fw, final=(l == DEPTH - 1))

    y_prompt = x[:N_CTX_TOK].reshape(BATCH, SEQ, D_MODEL)
    y_sample = x[N_CTX_TOK:].reshape(DEC_BATCH, DEC_SEQ, D_MODEL)
    return (y_prompt, y_sample, jnp.stack(ak_l, axis=1), jnp.stack(av_l, axis=1), jnp.stack(nk_l, axis=1),
            jnp.stack(nv_l, axis=1), jnp.stack(sd_l, axis=1), jnp.stack(sh_l, axis=1))
```

```python
import functools
import math

import jax
import jax.numpy as jnp
import numpy as np
from jax import lax
from jax.experimental import pallas as pl
from jax.experimental.pallas import tpu as pltpu

F32 = jnp.float32
BF16 = jnp.bfloat16

D_MODEL = 1024
BATCH = 16
SEQ = 256
DEPTH = 4
DEC_BATCH = 8
DEC_SEQ = 1024
PAST_LEN = 512
GRID_W = 64
HEAD_DIM = 64
BRANCH_WIDTH = 512
N_BRANCH = 4
A_HEADS = 8
A_KV_HEADS = 2
A_GROUP = 4
A_WINDOW = 128
A_BLOCK = 128
Q_BLOCK = 128
B_HEADS = 8
DELTA_CHUNK = 64
CONV_K = 5
C_HEADS = 8
HGRN_CHUNK = 16
D_HEADS = 8
NH_ROWS = 8
NH_COLS = 16
NH_QCOLS = 16
NH_KCOLS = 32
D_FF = 4 * D_MODEL
ROPE_BASE = 10000.0
ATTN_SCALE = HEAD_DIM ** -0.5
EPS = 1e-6
NEG = -1e30

N_CTX_TOK = BATCH * SEQ
N_LAT_TOK = DEC_BATCH * DEC_SEQ
N_TOK = N_CTX_TOK + N_LAT_TOK
N_MOD_ROWS = 1 + DEC_BATCH

_REF_COLS = (("a_q", 512), ("a_k", 128), ("a_v", 128), ("b_q", 512), ("b_k", 512), ("b_v", 512),
             ("b_z", 512), ("b_ab", 32), ("c_q", 512), ("c_f", 1024), ("c_i", 512), ("c_g", 512),
             ("d_q", 512), ("d_k", 512), ("d_v", 512), ("g", 4096))
_P_ORDER = ("g", "a_q", "b_q", "b_k", "b_v", "b_z", "c_q", "c_f", "c_i", "c_g", "d_q", "d_k", "d_v",
            "a_k", "a_v", "b_ab")
LANE = 128
N_P = 11264


def _layout():
    ref_off, o = {}, 0
    for name, w in _REF_COLS:
        ref_off[name] = (o, w)
        o += w
    p_off, o = {}, 0
    for name in _P_ORDER:
        w = ref_off[name][1]
        p_off[name] = (o, w)
        o += -(-w // LANE) * LANE
    assert o <= N_P
    return ref_off, p_off


REF_OFF, P_OFF = _layout()

VMEM_LIMIT = 56 * 1024 * 1024


def _mod_row(i, tm):
    nct = N_CTX_TOK // tm
    tpl = DEC_SEQ // tm
    return jnp.where(i < nct, 0, 1 + (i - nct) // tpl)


def _adaln_kernel(c_ref, w_ref, b_ref, o_ref):
    c = c_ref[...]
    s = c * jax.nn.sigmoid(c)
    o_ref[0] = jnp.dot(s, w_ref[0], preferred_element_type=F32) + b_ref[0]


def _adaln(cvec, ada_w, ada_b):
    tn = 1536
    n6 = 6 * D_MODEL
    rows = cvec.shape[0]
    return pl.pallas_call(
        _adaln_kernel,
        grid=(DEPTH, n6 // tn),
        in_specs=[pl.BlockSpec((rows, D_MODEL), lambda l, j: (0, 0)),
                  pl.BlockSpec((1, D_MODEL, tn), lambda l, j: (l, 0, j)),
                  pl.BlockSpec((1, 1, tn), lambda l, j: (l, 0, j))],
        out_specs=pl.BlockSpec((1, rows, tn), lambda l, j: (l, 0, j)),
        out_shape=jax.ShapeDtypeStruct((DEPTH, rows, n6), F32),
        compiler_params=pltpu.CompilerParams(dimension_semantics=("arbitrary", "arbitrary"),
                                             vmem_limit_bytes=VMEM_LIMIT),
        name="adaln",
    )(cvec, ada_w, ada_b.reshape(DEPTH, 1, n6))


ROW_CHUNK = 128


def _norm_mod_to(h_ref, x_ref, nw_ref, shift, scale):
    n = x_ref.shape[0] // ROW_CHUNK

    def body(r, carry):
        rows = pl.ds(pl.multiple_of(r * ROW_CHUNK, ROW_CHUNK), ROW_CHUNK)
        x = x_ref[rows, :]
        y = x * lax.rsqrt(jnp.mean(x * x, axis=-1, keepdims=True) + EPS) * nw_ref[...]
        h_ref[rows, :] = (y * (1.0 + scale) + shift).astype(BF16)
        return carry

    lax.fori_loop(0, n, body, 0)


def _inproj_kernel(x_ref, mod_ref, nw_ref, w_ref, o_ref, h_ref):
    @pl.when(pl.program_id(1) == 0)
    def _():
        _norm_mod_to(h_ref, x_ref, nw_ref, mod_ref[0, 0:1, :], mod_ref[0, 1:2, :])

    o_ref[...] = jnp.dot(h_ref[...], w_ref[...], preferred_element_type=F32)


def _inproj(x, mod, nw, w):
    tm, tn = 1024, 1024
    return pl.pallas_call(
        _inproj_kernel,
        grid=(N_TOK // tm, N_P // tn),
        in_specs=[pl.BlockSpec((tm, D_MODEL), lambda i, j: (i, 0)),
                  pl.BlockSpec((1, 6, D_MODEL), lambda i, j: (_mod_row(i, tm), 0, 0)),
                  pl.BlockSpec((1, D_MODEL), lambda i, j: (0, 0)),
                  pl.BlockSpec((D_MODEL, tn), lambda i, j: (0, j))],
        out_specs=pl.BlockSpec((tm, tn), lambda i, j: (i, j)),
        out_shape=jax.ShapeDtypeStruct((N_TOK, N_P), F32),
        scratch_shapes=[pltpu.VMEM((tm, D_MODEL), BF16)],
        compiler_params=pltpu.CompilerParams(dimension_semantics=("arbitrary", "arbitrary"),
                                             vmem_limit_bytes=VMEM_LIMIT),
        name="inproj",
    )(x, mod, nw, w)


MERGE_TM = 256
MERGE_CTX_TILES = N_CTX_TOK // MERGE_TM


def _merge_kernel(x_ref, *refs):
    y_refs, (g_ref, mod_ref, wb_ref, wo_ref, o_ref) = refs[:2 * N_BRANCH], refs[2 * N_BRANCH:]
    is_ctx = pl.program_id(0) < MERGE_CTX_TILES
    merged = None
    for k in range(N_BRANCH):
        y = jnp.where(is_ctx, y_refs[2 * k][...], y_refs[2 * k + 1][...])
        yp = jnp.dot(y.astype(BF16), wb_ref[k], preferred_element_type=F32)
        t = jax.nn.sigmoid(g_ref[:, k * D_MODEL:(k + 1) * D_MODEL]) * yp
        merged = t if merged is None else merged + t
    o = jnp.dot(merged.astype(BF16), wo_ref[...], preferred_element_type=F32)
    o_ref[...] = x_ref[...] + mod_ref[0, 2:3, :] * o


def _merge(x, ys, p, mod, wb, wo):
    tm = MERGE_TM
    nct = MERGE_CTX_TILES
    cspec = pl.BlockSpec((tm, BRANCH_WIDTH), lambda i: (jnp.minimum(i, nct - 1), 0))
    lspec = pl.BlockSpec((tm, BRANCH_WIDTH), lambda i: (jnp.maximum(i - nct, 0), 0))
    return pl.pallas_call(
        _merge_kernel,
        grid=(N_TOK // tm,),
        in_specs=[pl.BlockSpec((tm, D_MODEL), lambda i: (i, 0))]
        + [cspec, lspec] * N_BRANCH
        + [pl.BlockSpec((tm, N_BRANCH * D_MODEL), lambda i: (i, 0)),
           pl.BlockSpec((1, 6, D_MODEL), lambda i: (_mod_row(i, tm), 0, 0)),
           pl.BlockSpec((N_BRANCH, BRANCH_WIDTH, D_MODEL), lambda i: (0, 0, 0)),
           pl.BlockSpec((D_MODEL, D_MODEL), lambda i: (0, 0))],
        out_specs=pl.BlockSpec((tm, D_MODEL), lambda i: (i, 0)),
        out_shape=jax.ShapeDtypeStruct((N_TOK, D_MODEL), F32),
        compiler_params=pltpu.CompilerParams(dimension_semantics=("arbitrary",),
                                             vmem_limit_bytes=VMEM_LIMIT),
        name="merge",
    )(x, *[y for pair in ys for y in pair], p, mod, wb, wo)


def _mlp_kernel(x_ref, mod_ref, nw_ref, w1_ref, w2_ref, fw_ref, o_ref, h_ref, acc_ref, *, final):
    f = pl.program_id(1)

    @pl.when(f == 0)
    def _():
        _norm_mod_to(h_ref, x_ref, nw_ref, mod_ref[0, 3:4, :], mod_ref[0, 4:5, :])

    a = jnp.dot(h_ref[...], w1_ref[...], preferred_element_type=F32)
    a = jnp.square(jnp.maximum(a, 0.0)).astype(BF16)
    contrib = jnp.dot(a, w2_ref[...], preferred_element_type=F32)

    @pl.when(f == 0)
    def _():
        acc_ref[...] = contrib

    @pl.when(f != 0)
    def _():
        acc_ref[...] += contrib

    @pl.when(f == pl.num_programs(1) - 1)
    def _():
        y = x_ref[...] + mod_ref[0, 5:6, :] * acc_ref[...]
        if final:
            y = y * lax.rsqrt(jnp.mean(y * y, axis=-1, keepdims=True) + EPS) * fw_ref[...]
        o_ref[...] = y


def _mlp(x, mod, nw, w1, w2, fw, final):
    tm, tf = 1024, 512
    return pl.pallas_call(
        functools.partial(_mlp_kernel, final=final),
        grid=(N_TOK // tm, D_FF // tf),
        in_specs=[pl.BlockSpec((tm, D_MODEL), lambda i, f: (i, 0)),
                  pl.BlockSpec((1, 6, D_MODEL), lambda i, f: (_mod_row(i, tm), 0, 0)),
                  pl.BlockSpec((1, D_MODEL), lambda i, f: (0, 0)),
                  pl.BlockSpec((D_MODEL, tf), lambda i, f: (0, f)),
                  pl.BlockSpec((tf, D_MODEL), lambda i, f: (f, 0)),
                  pl.BlockSpec((1, D_MODEL), lambda i, f: (0, 0))],
        out_specs=pl.BlockSpec((tm, D_MODEL), lambda i, f: (i, 0)),
        out_shape=jax.ShapeDtypeStruct((N_TOK, D_MODEL), F32),
        scratch_shapes=[pltpu.VMEM((tm, D_MODEL), BF16), pltpu.VMEM((tm, D_MODEL), F32)],
        compiler_params=pltpu.CompilerParams(dimension_semantics=("arbitrary", "arbitrary"),
                                             vmem_limit_bytes=VMEM_LIMIT),
        name="mlp",
    )(x, mod, nw, w1, w2, fw)


HD = HEAD_DIM
N_HD = 2 * B_HEADS
CONV_PAD = 8


def _dot_nt(a, b):
    return lax.dot_general(a, b, (((1,), (1,)), ((), ())), preferred_element_type=F32)


def _dot_tn(a, b):
    return lax.dot_general(a, b, (((0,), (0,)), ((), ())), preferred_element_type=F32)


def _dot(a, b):
    return jnp.dot(a, b, preferred_element_type=F32)


def _split(x):
    hi = x.astype(BF16)
    return hi, (x - hi.astype(F32)).astype(BF16)


def _dot3(a, b):
    return _dot(a[0], b[0]) + (_dot(a[0], b[1]) + _dot(a[1], b[0]))


def _softplus(x):
    return jnp.maximum(x, 0.0) + jnp.log1p(jnp.exp(-jnp.abs(x)))


def _delta_kernel(*refs, L, has_s0):
    if has_s0:
        (q_ref, k_ref, v_ref, z_ref, ab_ref, cw_ref, prm_ref, nw_ref, s0_ref, y_ref,
         xpad, qkv_s, g_s, b_s, o_s, st_s) = refs
        sout_ref = None
    else:
        (q_ref, k_ref, v_ref, z_ref, ab_ref, cw_ref, prm_ref, nw_ref, y_ref, sout_ref,
         xpad, qkv_s, g_s, b_s, o_s, st_s) = refs
        s0_ref = None
    C = DELTA_CHUNK
    n_chunks = L // C

    zeros_pad = jnp.zeros((CONV_PAD, BRANCH_WIDTH), F32)
    xpad[0:CONV_PAD, :] = zeros_pad
    xpad[CONV_PAD + L:2 * CONV_PAD + L, :] = zeros_pad
    for idx, src in enumerate((q_ref, k_ref, v_ref)):
        xpad[CONV_PAD:CONV_PAD + L, :] = src[...]
        for r in range(n_chunks):
            acc = None
            for j in range(CONV_K):
                start = CONV_PAD + r * C + j - CONV_K // 2
                t = xpad[start:start + C, :] * cw_ref[j:j + 1, idx * BRANCH_WIDTH:(idx + 1) * BRANCH_WIDTH]
                acc = t if acc is None else acc + t
            y = acc * jax.nn.sigmoid(acc)
            if idx == 2:
                qkv_s[idx, r * C:(r + 1) * C, :] = y
            else:
                for h in range(B_HEADS):
                    yh = y[:, h * HD:(h + 1) * HD]
                    yh = yh * lax.rsqrt(jnp.sum(yh * yh, axis=-1, keepdims=True) + EPS)
                    if idx == 0:
                        yh = yh * ATTN_SCALE
                    qkv_s[idx, r * C:(r + 1) * C, h * HD:(h + 1) * HD] = yh

    ab = ab_ref[...]
    g_s[...] = -jnp.exp(prm_ref[0:1, :]) * _softplus(ab + prm_ref[1:2, :])
    b_s[...] = jax.nn.sigmoid(ab)

    if has_s0:
        st_s[...] = s0_ref[0]
    else:
        st_s[...] = jnp.zeros_like(st_s)

    ri = lax.broadcasted_iota(jnp.int32, (C, C), 0)
    ci = lax.broadcasted_iota(jnp.int32, (C, C), 1)
    eye = (ri == ci).astype(F32)
    level_masks = [(ri // 2) == (ci // 2)]
    blk = 2
    while blk < C:
        level_masks.append(((ri // (2 * blk)) == (ci // (2 * blk))) & ((ri // blk) != (ci // blk)))
        blk *= 2

    def process(chunk, d):
        rows = pl.ds(pl.multiple_of(chunk * C, C), C)
        incl = (ri >= ci) if d == 0 else (ri <= ci)
        strict = (ri > ci) if d == 0 else (ri < ci)
        g = g_s[rows, :]
        beta = b_s[rows, :]
        gc = jnp.dot(incl.astype(F32), g, precision=lax.Precision.HIGHEST, preferred_element_type=F32)
        gct = gc.T
        tot = gc[C - 1:C, :] if d == 0 else gc[0:1, :]
        eg = jnp.exp(gc)
        ek = jnp.exp(tot - gc)
        etot = jnp.exp(tot)
        for h in range(B_HEADS):
            c = d * B_HEADS + h
            hs = slice(h * HD, (h + 1) * HD)
            qh = qkv_s[0, rows, hs]
            kh = qkv_s[1, rows, hs]
            vh = qkv_s[2, rows, hs]
            bcol = beta[:, N_HD + c:N_HD + c + 1]
            decay = jnp.exp(jnp.where(incl, gc[:, c:c + 1] - gct[c:c + 1, :], NEG))
            qk = _dot_nt(jnp.concatenate([qh, kh], axis=0), kh)
            pqk = qk[:C] * decay
            amat = jnp.where(strict, bcol * qk[C:] * decay, 0.0)
            rhs = jnp.concatenate([bcol * vh, (bcol * eg[:, c:c + 1]) * kh], axis=1)
            tinv = eye - jnp.where(level_masks[0], amat, 0.0)
            for lm in level_masks[1:]:
                ts = _split(tinv)
                et = _dot3(_split(jnp.where(lm, amat, 0.0)), ts)
                tinv = tinv - _dot3(ts, _split(et))
            sol = _dot3(_split(tinv), _split(rhs))
            wv = sol[:, :HD]
            wk = sol[:, HD:]
            s = st_s[c]
            t = _dot(jnp.concatenate([qh * eg[:, c:c + 1], wk], axis=0), s)
            u = wv - t[C:]
            o_s[d, rows, hs] = t[:C] + _dot(pqk, u)
            st_s[c] = etot[:, c:c + 1] * s + _dot_tn(kh * ek[:, c:c + 1], u)

    def body(n, carry):
        process(n, 0)
        process(n_chunks - 1 - n, 1)
        return carry

    lax.fori_loop(0, n_chunks, body, 0)

    if not has_s0:
        sout_ref[0] = st_s[...]

    def out_body(r, carry):
        rows = pl.ds(pl.multiple_of(r * C, C), C)
        z = z_ref[rows, :]
        gate = z * jax.nn.sigmoid(z) * nw_ref[...]
        for h in range(B_HEADS):
            hs = slice(h * HD, (h + 1) * HD)
            o = o_s[0, rows, hs] + o_s[1, rows, hs]
            o = o * lax.rsqrt(jnp.mean(o * o, axis=-1, keepdims=True) + EPS)
            y_ref[rows, hs] = o * gate[:, hs]
        return carry

    lax.fori_loop(0, n_chunks, out_body, 0)


def _delta(p, cw, prm, nw, s0, *, L, n_seq, row_block0):
    has_s0 = s0 is not None

    def pspec(name):
        off, w = P_OFF[name]
        return pl.BlockSpec((L, w if w >= LANE else LANE), lambda i: (row_block0 + i, off // max(w, LANE)))

    in_specs = [pspec("b_q"), pspec("b_k"), pspec("b_v"), pspec("b_z"), pspec("b_ab"),
                pl.BlockSpec((CONV_K, 3 * BRANCH_WIDTH), lambda i: (0, 0)),
                pl.BlockSpec((8, LANE), lambda i: (0, 0)),
                pl.BlockSpec((1, BRANCH_WIDTH), lambda i: (0, 0))]
    args = [p, p, p, p, p, cw, prm, nw]
    st_spec = pl.BlockSpec((1, N_HD, HD, HD), lambda i: (i, 0, 0, 0))
    y_shape = jax.ShapeDtypeStruct((n_seq * L, BRANCH_WIDTH), F32)
    y_spec = pl.BlockSpec((L, BRANCH_WIDTH), lambda i: (i, 0))
    if has_s0:
        in_specs.append(st_spec)
        args.append(s0)
        out_specs, out_shape = y_spec, y_shape
    else:
        out_specs = (y_spec, st_spec)
        out_shape = (y_shape, jax.ShapeDtypeStruct((n_seq, N_HD, HD, HD), F32))
    return pl.pallas_call(
        functools.partial(_delta_kernel, L=L, has_s0=has_s0),
        grid=(n_seq,),
        in_specs=in_specs,
        out_specs=out_specs,
        out_shape=out_shape,
        scratch_shapes=[pltpu.VMEM((L + 2 * CONV_PAD, BRANCH_WIDTH), F32),
                        pltpu.VMEM((3, L, BRANCH_WIDTH), F32),
                        pltpu.VMEM((L, LANE), F32),
                        pltpu.VMEM((L, LANE), F32),
                        pltpu.VMEM((2, L, BRANCH_WIDTH), F32),
                        pltpu.VMEM((N_HD, HD, HD), F32)],
        compiler_params=pltpu.CompilerParams(dimension_semantics=("arbitrary",),
                                             vmem_limit_bytes=VMEM_LIMIT),
        name="delta_lat" if has_s0 else "delta_ctx",
    )(*args)


def _delta_params(a_log, dt_bias, norm_w):
    prm = jnp.zeros((8, LANE), F32)
    prm = prm.at[0, :N_HD].set(a_log.reshape(N_HD).astype(F32))
    prm = prm.at[1, :N_HD].set(dt_bias.reshape(N_HD).astype(F32))
    return prm, jnp.tile(norm_w.astype(F32), B_HEADS).reshape(1, BRANCH_WIDTH)


def _softmax_pv(s_list, v_list, sink_col):
    m = None
    for s in s_list:
        mi = jnp.max(s, axis=-1, keepdims=True)
        m = mi if m is None else jnp.maximum(m, mi)
    if sink_col is not None:
        m = jnp.maximum(m, sink_col)
    den = None
    acc = None
    for s, v in zip(s_list, v_list):
        p = jnp.exp(s - m)
        di = jnp.sum(p, axis=-1, keepdims=True)
        den = di if den is None else den + di
        ai = _dot(p, v)
        acc = ai if acc is None else acc + ai
    if sink_col is not None:
        den = den + jnp.exp(sink_col - m)
    return acc / den


def _sink_col(sink_ref, heads, rows):
    return jnp.concatenate([jnp.full((rows, 1), sink_ref[h], F32) for h in heads], axis=0)


def _ctx_attn_kernel(sink_ref, aq_ref, ak_ref, av_ref, dq_ref, dk_ref, dv_ref, ya_ref, yd_ref):
    L = aq_ref.shape[0]
    for j in range(A_KV_HEADS):
        js = slice(j * HD, (j + 1) * HD)
        heads = range(j * A_GROUP, (j + 1) * A_GROUP)
        q = jnp.concatenate([aq_ref[:, h * HD:(h + 1) * HD] for h in heads], axis=0) * ATTN_SCALE
        s = _dot_nt(q, ak_ref[:, js])
        o = _softmax_pv([s], [av_ref[:, js]], _sink_col(sink_ref, heads, L))
        for g, h in enumerate(heads):
            ya_ref[:, h * HD:(h + 1) * HD] = o[g * L:(g + 1) * L]
    for h in range(D_HEADS):
        hs = slice(h * HD, (h + 1) * HD)
        s = _dot_nt(dq_ref[:, hs] * ATTN_SCALE, dk_ref[:, hs])
        yd_ref[:, hs] = _softmax_pv([s], [dv_ref[:, hs]], None)


def _pspec(name, rows, row_block0):
    off, w = P_OFF[name]
    bw = max(w, LANE)
    return pl.BlockSpec((rows, bw), lambda i: (row_block0 + i, off // bw))


_SMEM_SPEC = pl.BlockSpec(memory_space=pltpu.SMEM)


def _ctx_attn(p, sink):
    y_shape = jax.ShapeDtypeStruct((N_CTX_TOK, BRANCH_WIDTH), F32)
    y_spec = pl.BlockSpec((SEQ, BRANCH_WIDTH), lambda i: (i, 0))
    return pl.pallas_call(
        _ctx_attn_kernel,
        grid=(BATCH,),
        in_specs=[_SMEM_SPEC] + [_pspec(n, SEQ, 0) for n in ("a_q", "a_k", "a_v", "d_q", "d_k", "d_v")],
        out_specs=(y_spec, y_spec),
        out_shape=(y_shape, y_shape),
        compiler_params=pltpu.CompilerParams(dimension_semantics=("arbitrary",),
                                             vmem_limit_bytes=VMEM_LIMIT),
        name="ctx_attn",
    )(sink, p, p, p, p, p, p)


def _rope_tables():
    t = np.arange(DEC_SEQ)
    quarter = HD // 4
    inv = ROPE_BASE ** (-np.arange(quarter, dtype=np.float64) / quarter)
    ang_r = (t // GRID_W)[:, None] * inv[None, :]
    ang_c = (t % GRID_W)[:, None] * inv[None, :]
    cos = np.concatenate([np.cos(ang_r)] * 2 + [np.cos(ang_c)] * 2, axis=1)
    sin = np.concatenate([-np.sin(ang_r), np.sin(ang_r), -np.sin(ang_c), np.sin(ang_c)], axis=1)
    return (jnp.asarray(np.tile(cos, (1, 2)), F32), jnp.asarray(np.tile(sin, (1, 2)), F32))


def _rope128(x, cos, sin):
    lane = lax.broadcasted_iota(jnp.int32, x.shape, 1)
    swapped = jnp.where((lane % 32) < 16, pltpu.roll(x, LANE - 16, 1), pltpu.roll(x, 16, 1))
    return x * cos + swapped * sin


def _win_attn_kernel(sink_ref, q_ref, k_ref, v_ref, ck_ref, cv_ref, cos_ref, sin_ref, y_ref, qr_s, kr_s):
    L = DEC_SEQ
    nb = L // A_BLOCK
    cos = cos_ref[...]
    sin = sin_ref[...]
    kr_s[...] = _rope128(k_ref[...], cos, sin)
    for c in range(BRANCH_WIDTH // LANE):
        cs = slice(c * LANE, (c + 1) * LANE)
        qr_s[:, cs] = _rope128(q_ref[:, cs], cos, sin) * ATTN_SCALE

    W = 3 * A_BLOCK
    qi = lax.broadcasted_iota(jnp.int32, (A_BLOCK, W), 0)
    ki = lax.broadcasted_iota(jnp.int32, (A_BLOCK, W), 1)

    def body(i, carry):
        start = jnp.clip(i - 1, 0, nb - 3) * A_BLOCK
        rows = pl.ds(pl.multiple_of(i * A_BLOCK, A_BLOCK), A_BLOCK)
        krows = pl.ds(pl.multiple_of(start, A_BLOCK), W)
        ok = jnp.abs(i * A_BLOCK + qi - (start + ki)) <= A_WINDOW
        bias = jnp.where(ok, 0.0, NEG)
        bias4 = jnp.concatenate([bias] * A_GROUP, axis=0)
        for j in range(A_KV_HEADS):
            js = slice(j * HD, (j + 1) * HD)
            heads = range(j * A_GROUP, (j + 1) * A_GROUP)
            q = jnp.concatenate([qr_s[rows, h * HD:(h + 1) * HD] for h in heads], axis=0)
            s_loc = _dot_nt(q, kr_s[krows, js]) + bias4
            s_ctx = _dot_nt(q, ck_ref[0, 0, :, js])
            o = _softmax_pv([s_loc, s_ctx], [v_ref[krows, js], cv_ref[0, 0, :, js]],
                            _sink_col(sink_ref, heads, A_BLOCK))
            for g, h in enumerate(heads):
                y_ref[rows, h * HD:(h + 1) * HD] = o[g * A_BLOCK:(g + 1) * A_BLOCK]
        return carry

    lax.fori_loop(0, nb, body, 0)


def _win_attn(p, sink, cache_k, cache_v, layer, cos, sin):
    row0 = N_CTX_TOK // DEC_SEQ
    cspec = pl.BlockSpec((1, 1, PAST_LEN, LANE), lambda i: (i, layer, 0, 0))
    tspec = pl.BlockSpec((DEC_SEQ, LANE), lambda i: (0, 0))
    return pl.pallas_call(
        _win_attn_kernel,
        grid=(DEC_BATCH,),
        in_specs=[_SMEM_SPEC] + [_pspec(n, DEC_SEQ, row0) for n in ("a_q", "a_k", "a_v")]
        + [cspec, cspec, tspec, tspec],
        out_specs=pl.BlockSpec((DEC_SEQ, BRANCH_WIDTH), lambda i: (i, 0)),
        out_shape=jax.ShapeDtypeStruct((N_LAT_TOK, BRANCH_WIDTH), F32),
        scratch_shapes=[pltpu.VMEM((DEC_SEQ, BRANCH_WIDTH), F32), pltpu.VMEM((DEC_SEQ, LANE), F32)],
        compiler_params=pltpu.CompilerParams(dimension_semantics=("arbitrary",),
                                             vmem_limit_bytes=VMEM_LIMIT),
        name="win_attn",
    )(sink, p, p, p, cache_k, cache_v, cos, sin)


N_GRID_ROWS = DEC_SEQ // GRID_W
N_DR = 2 * NH_ROWS - 1


def _na_bias_table(rpb):
    qc = np.arange(GRID_W)[:, None]
    kc = np.arange(GRID_W)[None, :]
    wstart = np.clip(qc - NH_COLS // 2, 0, GRID_W - NH_COLS)
    ok = (kc >= wstart) & (kc < wstart + NH_COLS)
    dc = np.clip(kc - qc + NH_COLS - 1, 0, 2 * NH_COLS - 2)
    t = jnp.where(ok[None, None], rpb.astype(F32)[:, :, dc], NEG)
    return jnp.concatenate([t[:, :-1], t[:, 1:]], axis=-1)


def _na_attn_kernel(q_ref, k_ref, v_ref, ck_ref, cv_ref, t_ref, y_ref):
    kh = NH_ROWS
    n_loc = kh * GRID_W

    def body(r, carry):
        rs = jnp.clip(r - kh // 2, 0, N_GRID_ROWS - kh)
        rows = pl.ds(pl.multiple_of(r * GRID_W, GRID_W), GRID_W)
        krows = pl.ds(pl.multiple_of(rs * GRID_W, GRID_W), n_loc)
        s0 = rs - r + NH_ROWS - 1
        for h in range(D_HEADS):
            hs = slice(h * HD, (h + 1) * HD)
            q = q_ref[rows, hs] * ATTN_SCALE
            bias = jnp.concatenate([t_ref[h, s0 + 2 * w] for w in range(kh // 2)], axis=1)
            s_loc = _dot_nt(q, k_ref[krows, hs]) + bias
            s_ctx = _dot_nt(q, ck_ref[0, 0, :, hs])
            y_ref[rows, hs] = _softmax_pv([s_loc, s_ctx], [v_ref[krows, hs], cv_ref[0, 0, :, hs]], None)
        return carry

    lax.fori_loop(0, N_GRID_ROWS, body, 0)


def _na_attn(p, cache_k, cache_v, layer, table):
    row0 = N_CTX_TOK // DEC_SEQ
    cspec = pl.BlockSpec((1, 1, PAST_LEN, BRANCH_WIDTH), lambda i: (i, layer, 0, 0))
    return pl.pallas_call(
        _na_attn_kernel,
        grid=(DEC_BATCH,),
        in_specs=[_pspec(n, DEC_SEQ, row0) for n in ("d_q", "d_k", "d_v")]
        + [cspec, cspec, pl.BlockSpec((D_HEADS, N_DR - 1, GRID_W, LANE), lambda i: (0, 0, 0, 0))],
        out_specs=pl.BlockSpec((DEC_SEQ, BRANCH_WIDTH), lambda i: (i, 0)),
        out_shape=jax.ShapeDtypeStruct((N_LAT_TOK, BRANCH_WIDTH), F32),
        compiler_params=pltpu.CompilerParams(dimension_semantics=("arbitrary",),
                                             vmem_limit_bytes=VMEM_LIMIT),
        name="na_attn",
    )(p, p, p, cache_k, cache_v, table)


HG = 128
N_PAIR = C_HEADS // 2
CPG = HG // HGRN_CHUNK


def _hgrn_sel():
    j = np.arange(HGRN_CHUNK)[:, None, None, None]
    lane = np.arange(LANE)[None, :, None, None]
    e = np.arange(2)[None, None, :, None]
    c = np.arange(LANE)[None, None, None, :]
    sel = ((lane // HD) == e) & ((c % HGRN_CHUNK) == j)
    return jnp.asarray(sel.reshape(HGRN_CHUNK * LANE, 2 * LANE), BF16)


def _hgrn_kernel(*refs, L, has_s0):
    if has_s0:
        (q_ref, f_ref, i_ref, g_ref, lb_ref, nw_ref, sel_ref, s0_ref, y_ref,
         lf_s, ck_s, qs_s, o_s, st_s, zc_s) = refs
        sout_ref = None
    else:
        (q_ref, f_ref, i_ref, g_ref, lb_ref, nw_ref, sel_ref, y_ref, sout_ref,
         lf_s, ck_s, qs_s, o_s, st_s, zc_s) = refs
        s0_ref = None
    C = HGRN_CHUNK
    n_groups = L // HG
    R = 64

    lb = lb_ref[...]
    log_lb = jnp.log(lb)
    log_1mlb = jnp.log1p(-lb)

    def pre_body(r, carry):
        rows = pl.ds(pl.multiple_of(r * R, R), R)
        cf = f_ref[rows, :]
        b = log_1mlb - _softplus(-cf)
        lf_s[rows, :] = jnp.maximum(log_lb, b) + jnp.log1p(jnp.exp(-jnp.abs(log_lb - b)))
        ck_s[rows, :] = (1.0 - lb) * jax.nn.sigmoid(-cf)
        cq = q_ref[rows, :]
        qs_s[rows, :] = cq * jax.nn.sigmoid(cq)
        return carry

    lax.fori_loop(0, L // R, pre_body, 0)

    if has_s0:
        st_s[...] = s0_ref[0]
    else:
        st_s[...] = jnp.zeros_like(st_s)

    ri = lax.broadcasted_iota(jnp.int32, (HG, HG), 0)
    ci = lax.broadcasted_iota(jnp.int32, (HG, HG), 1)
    same_chunk = (ri // C) == (ci // C)
    same_head = (ri // HD) == (ci // HD)
    tl = ri % C

    for d in range(2):
        cum_mat = (same_chunk & ((ci <= ri) if d == 0 else (ci >= ri))).astype(F32)
        for hp in range(N_PAIR):
            sidx = d * N_PAIR + hp
            fcols = slice(d * BRANCH_WIDTH + hp * LANE, d * BRANCH_WIDTH + (hp + 1) * LANE)
            hcols = slice(hp * LANE, (hp + 1) * LANE)

            def body(n, carry, d=d, cum_mat=cum_mat, sidx=sidx, fcols=fcols, hcols=hcols):
                gi = n if d == 0 else n_groups - 1 - n
                rows = pl.ds(pl.multiple_of(gi * HG, HG), HG)
                lf = lf_s[rows, fcols]
                kk = ck_s[rows, fcols]
                qq = qs_s[rows, hcols]
                vv = i_ref[rows, hcols]
                bcum = jnp.dot(cum_mat, lf, precision=lax.Precision.HIGHEST, preferred_element_type=F32)
                b3 = bcum.reshape(CPG, C, LANE)
                k3 = kk.reshape(CPG, C, LANE)
                for j in range(C):
                    bj = jnp.broadcast_to(b3[:, j:j + 1, :], (CPG, C, LANE)).reshape(HG, LANE)
                    kj = jnp.broadcast_to(k3[:, j:j + 1, :], (CPG, C, LANE)).reshape(HG, LANE)
                    ok = (tl >= j) if d == 0 else (tl <= j)
                    z = qq * jnp.exp(jnp.where(ok, bcum - bj, NEG)) * kj
                    zc_s[:, j * LANE:(j + 1) * LANE] = z.astype(BF16)
                att = _dot(zc_s[...], sel_ref[...])
                o_intra = jnp.concatenate(
                    [_dot(jnp.where(same_chunk, att[:, e * LANE:(e + 1) * LANE], 0.0), vv[:, e * HD:(e + 1) * HD])
                     for e in range(2)], axis=1)
                o_inter = [None] * CPG
                for cix in range(CPG):
                    c = cix if d == 0 else CPG - 1 - cix
                    r16 = slice(c * C, (c + 1) * C)
                    bc = bcum[r16]
                    blast = bc[C - 1:C] if d == 0 else bc[0:1]
                    st = st_s[sidx]
                    o_inter[c] = _dot_nt(qq[r16] * jnp.exp(bc), st)
                    upd = _dot_tn(vv[r16], kk[r16] * jnp.exp(blast - bc))
                    st_s[sidx] = st * jnp.exp(blast) + jnp.where(same_head, upd, 0.0)
                o_s[d, rows, hcols] = o_intra + jnp.concatenate(o_inter, axis=0)
                return carry

            lax.fori_loop(0, n_groups, body, 0)

    if not has_s0:
        sout_ref[0] = st_s[...]

    def out_body(r, carry):
        rows = pl.ds(pl.multiple_of(r * R, R), R)
        gate = jax.nn.sigmoid(g_ref[rows, :])
        for h in range(C_HEADS):
            hs = slice(h * HD, (h + 1) * HD)
            o = (o_s[0, rows, hs] + o_s[1, rows, hs]) * gate[:, hs]
            y_ref[rows, hs] = o * lax.rsqrt(jnp.mean(o * o, axis=-1, keepdims=True) + EPS) * nw_ref[:, hs]
        return carry

    lax.fori_loop(0, L // R, out_body, 0)


def _hgrn(p, lb, nw, sel, s0, *, L, n_seq, row_block0):
    has_s0 = s0 is not None
    in_specs = [_pspec(n, L, row_block0) for n in ("c_q", "c_f", "c_i", "c_g")] + [
        pl.BlockSpec((1, 2 * BRANCH_WIDTH), lambda i: (0, 0)),
        pl.BlockSpec((1, BRANCH_WIDTH), lambda i: (0, 0)),
        pl.BlockSpec((HGRN_CHUNK * LANE, 2 * LANE), lambda i: (0, 0))]
    args = [p, p, p, p, lb, nw, sel]
    st_spec = pl.BlockSpec((1, 2 * N_PAIR, LANE, LANE), lambda i: (i, 0, 0, 0))
    y_shape = jax.ShapeDtypeStruct((n_seq * L, BRANCH_WIDTH), F32)
    y_spec = pl.BlockSpec((L, BRANCH_WIDTH), lambda i: (i, 0))
    if has_s0:
        in_specs.append(st_spec)
        args.append(s0)
        out_specs, out_shape = y_spec, y_shape
    else:
        out_specs = (y_spec, st_spec)
        out_shape = (y_shape, jax.ShapeDtypeStruct((n_seq, 2 * N_PAIR, LANE, LANE), F32))
    return pl.pallas_call(
        functools.partial(_hgrn_kernel, L=L, has_s0=has_s0),
        grid=(n_seq,),
        in_specs=in_specs,
        out_specs=out_specs,
        out_shape=out_shape,
        scratch_shapes=[pltpu.VMEM((L, 2 * BRANCH_WIDTH), F32),
                        pltpu.VMEM((L, 2 * BRANCH_WIDTH), F32),
                        pltpu.VMEM((L, BRANCH_WIDTH), F32),
                        pltpu.VMEM((2, L, BRANCH_WIDTH), F32),
                        pltpu.VMEM((2 * N_PAIR, LANE, LANE), F32),
                        pltpu.VMEM((HG, HGRN_CHUNK * LANE), BF16)],
        compiler_params=pltpu.CompilerParams(dimension_semantics=("arbitrary",),
                                             vmem_limit_bytes=VMEM_LIMIT),
        name="hgrn_lat" if has_s0 else "hgrn_ctx",
    )(*args)


def _hgrn_state_in(s):
    b = s.shape[0]
    st = jnp.swapaxes(s.astype(F32), -1, -2).reshape(b, 2, N_PAIR, 2, HD, HD)
    z = jnp.zeros_like(st[:, :, :, 0])
    top = jnp.concatenate([st[:, :, :, 0], z], axis=-1)
    bot = jnp.concatenate([z, st[:, :, :, 1]], axis=-1)
    return jnp.concatenate([top, bot], axis=-2).reshape(b, 2 * N_PAIR, LANE, LANE)


def _hgrn_state_out(s):
    b = s.shape[0]
    s = s.reshape(b, 2, N_PAIR, LANE, LANE)
    blocks = jnp.stack([s[..., :HD, :HD], s[..., HD:, HD:]], axis=3)
    return jnp.swapaxes(blocks, -1, -2).reshape(b, 2, C_HEADS, HD, HD)


def _pcol(p, name):
    o, w = P_OFF[name]
    return p[..., o:o + w]


def _pack_w_in(w_in):
    parts = []
    used = 0
    for name in _P_ORDER:
        o, w = REF_OFF[name]
        parts.append(w_in[..., o:o + w])
        pw = -(-w // LANE) * LANE
        if pw != w:
            parts.append(jnp.zeros(w_in.shape[:-1] + (pw - w,), w_in.dtype))
        used += pw
    parts.append(jnp.zeros(w_in.shape[:-1] + (N_P - used,), w_in.dtype))
    return jnp.concatenate(parts, axis=-1).astype(BF16)


def kernel(x_prompt, x_sample, cache_attn_k, cache_attn_v, cache_na_k, cache_na_v, state_delta, state_hgrn,
           c, c_ctx, norm_w, ada_w, ada_b, w_in, attn_sink, delta_conv, delta_a_log, delta_dt_bias,
           delta_norm_w, hgrn_lb, hgrn_norm_w, na_rpb, w_branch, w_out, mlp_w1, mlp_w2, final_norm_w):
    lb = jnp.cumsum(jax.nn.softmax(hgrn_lb.astype(F32), axis=0), axis=0)
    lb = lb - lb[:1]

    cvec = jnp.concatenate([c_ctx[None, :], c, jnp.zeros((16 - N_MOD_ROWS, D_MODEL), F32)], axis=0)
    mod = _adaln(cvec, ada_w, ada_b).reshape(DEPTH, 16, 6, D_MODEL)

    w_in_p = _pack_w_in(w_in)
    wb = w_branch.astype(BF16)
    wo = w_out.astype(BF16)
    w1 = mlp_w1.astype(BF16)
    w2 = mlp_w2.astype(BF16)
    fw = final_norm_w.reshape(1, D_MODEL)

    cos, sin = _rope_tables()
    sel = _hgrn_sel()
    cak = cache_attn_k.reshape(DEC_BATCH, DEPTH, PAST_LEN, A_KV_HEADS * HD)
    cav = cache_attn_v.reshape(DEC_BATCH, DEPTH, PAST_LEN, A_KV_HEADS * HD)
    cnk = cache_na_k.reshape(DEC_BATCH, DEPTH, PAST_LEN, BRANCH_WIDTH)
    cnv = cache_na_v.reshape(DEC_BATCH, DEPTH, PAST_LEN, BRANCH_WIDTH)
    lat_blk0 = N_CTX_TOK // DEC_SEQ

    x = jnp.concatenate([x_prompt.reshape(N_CTX_TOK, D_MODEL), x_sample.reshape(N_LAT_TOK, D_MODEL)], axis=0)
    ak_l, av_l, nk_l, nv_l, sd_l, sh_l = [], [], [], [], [], []
    for l in range(DEPTH):
        p = _inproj(x, mod[l], norm_w[l, 0].reshape(1, D_MODEL), w_in_p[l])
        pc = p[:N_CTX_TOK]
        ak_l.append(_pcol(pc, "a_k").reshape(BATCH, SEQ, A_KV_HEADS, HD))
        av_l.append(_pcol(pc, "a_v").reshape(BATCH, SEQ, A_KV_HEADS, HD))
        nk_l.append(_pcol(pc, "d_k").reshape(BATCH, SEQ, D_HEADS, HD))
        nv_l.append(_pcol(pc, "d_v").reshape(BATCH, SEQ, D_HEADS, HD))

        ya_c, yd_c = _ctx_attn(p, attn_sink[l])
        ya_l = _win_attn(p, attn_sink[l], cak, cav, l, cos, sin)
        yd_l = _na_attn(p, cnk, cnv, l, _na_bias_table(na_rpb[l]))

        prm, dnw = _delta_params(delta_a_log[l], delta_dt_bias[l], delta_norm_w[l])
        yb_c, sd = _delta(p, delta_conv[l], prm, dnw, None, L=SEQ, n_seq=BATCH, row_block0=0)
        yb_l = _delta(p, delta_conv[l], prm, dnw, state_delta[:, l].reshape(DEC_BATCH, N_HD, HD, HD),
                      L=DEC_SEQ, n_seq=DEC_BATCH, row_block0=lat_blk0)
        sd_l.append(sd.reshape(BATCH, 2, B_HEADS, HD, HD))

        lbl = lb[l].reshape(1, 2 * BRANCH_WIDTH)
        hnw = jnp.tile(hgrn_norm_w[l].astype(F32), C_HEADS).reshape(1, BRANCH_WIDTH)
        yc_c, sh = _hgrn(p, lbl, hnw, sel, None, L=SEQ, n_seq=BATCH, row_block0=0)
        yc_l = _hgrn(p, lbl, hnw, sel, _hgrn_state_in(state_hgrn[:, l]),
                     L=DEC_SEQ, n_seq=DEC_BATCH, row_block0=lat_blk0)
        sh_l.append(_hgrn_state_out(sh))

        x = _merge(x, [(ya_c, ya_l), (yb_c, yb_l), (yc_c, yc_l), (yd_c, yd_l)], p, mod[l], wb[l], wo[l])
        x = _mlp(x, mod[l], norm_w[l, 1].reshape(1, D_MODEL), w1[l], w2[l], fw, final=(l == DEPTH - 1))

    y_prompt = x[:N_CTX_TOK].reshape(BATCH, SEQ, D_MODEL)
    y_sample = x[N_CTX_TOK:].reshape(DEC_BATCH, DEC_SEQ, D_MODEL)
    return (y_prompt, y_sample, jnp.stack(ak_l, axis=1), jnp.stack(av_l, axis=1), jnp.stack(nk_l, axis=1),
            jnp.stack(nv_l, axis=1), jnp.stack(sd_l, axis=1), jnp.stack(sh_l, axis=1))
```

```python
import functools
import math

import jax
import jax.numpy as jnp
import numpy as np
from jax import lax
from jax.experimental import pallas as pl
from jax.experimental.pallas import tpu as pltpu

F32 = jnp.float32
BF16 = jnp.bfloat16

D_MODEL = 1024
BATCH = 16
SEQ = 256
DEPTH = 4
DEC_BATCH = 8
DEC_SEQ = 1024
PAST_LEN = 512
GRID_W = 64
HEAD_DIM = 64
BRANCH_WIDTH = 512
N_BRANCH = 4
A_HEADS = 8
A_KV_HEADS = 2
A_GROUP = 4
A_WINDOW = 128
A_BLOCK = 128
Q_BLOCK = 128
B_HEADS = 8
DELTA_CHUNK = 64
CONV_K = 5
C_HEADS = 8
HGRN_CHUNK = 16
D_HEADS = 8
NH_ROWS = 8
NH_COLS = 16
NH_QCOLS = 16
NH_KCOLS = 32
D_FF = 4 * D_MODEL
ROPE_BASE = 10000.0
ATTN_SCALE = HEAD_DIM ** -0.5
EPS = 1e-6
NEG = -1e30

N_CTX_TOK = BATCH * SEQ
N_LAT_TOK = DEC_BATCH * DEC_SEQ
N_TOK = N_CTX_TOK + N_LAT_TOK
N_MOD_ROWS = 1 + DEC_BATCH

_REF_COLS = (("a_q", 512), ("a_k", 128), ("a_v", 128), ("b_q", 512), ("b_k", 512), ("b_v", 512),
             ("b_z", 512), ("b_ab", 32), ("c_q", 512), ("c_f", 1024), ("c_i", 512), ("c_g", 512),
             ("d_q", 512), ("d_k", 512), ("d_v", 512), ("g", 4096))
_P_ORDER = ("g", "a_q", "b_q", "b_k", "b_v", "b_z", "c_q", "c_f", "c_i", "c_g", "d_q", "d_k", "d_v",
            "a_k", "a_v", "b_ab")
LANE = 128
N_P = 11264


def _layout():
    ref_off, o = {}, 0
    for name, w in _REF_COLS:
        ref_off[name] = (o, w)
        o += w
    p_off, o = {}, 0
    for name in _P_ORDER:
        w = ref_off[name][1]
        p_off[name] = (o, w)
        o += -(-w // LANE) * LANE
    assert o <= N_P
    return ref_off, p_off


REF_OFF, P_OFF = _layout()

VMEM_LIMIT = 56 * 1024 * 1024


def _mod_row(i, tm):
    nct = N_CTX_TOK // tm
    tpl = DEC_SEQ // tm
    return jnp.where(i < nct, 0, 1 + (i - nct) // tpl)


def _adaln_kernel(c_ref, w_ref, b_ref, o_ref):
    c = c_ref[...]
    s = c * jax.nn.sigmoid(c)
    o_ref[0] = jnp.dot(s, w_ref[0], preferred_element_type=F32) + b_ref[0]


def _adaln(cvec, ada_w, ada_b):
    tn = 1536
    n6 = 6 * D_MODEL
    rows = cvec.shape[0]
    return pl.pallas_call(
        _adaln_kernel,
        grid=(DEPTH, n6 // tn),
        in_specs=[pl.BlockSpec((rows, D_MODEL), lambda l, j: (0, 0)),
                  pl.BlockSpec((1, D_MODEL, tn), lambda l, j: (l, 0, j)),
                  pl.BlockSpec((1, 1, tn), lambda l, j: (l, 0, j))],
        out_specs=pl.BlockSpec((1, rows, tn), lambda l, j: (l, 0, j)),
        out_shape=jax.ShapeDtypeStruct((DEPTH, rows, n6), F32),
        compiler_params=pltpu.CompilerParams(dimension_semantics=("arbitrary", "arbitrary"),
                                             vmem_limit_bytes=VMEM_LIMIT),
        name="adaln",
    )(cvec, ada_w, ada_b.reshape(DEPTH, 1, n6))


ROW_CHUNK = 128


def _norm_mod_to(h_ref, x_ref, nw_ref, shift, scale):
    n = x_ref.shape[0] // ROW_CHUNK

    def body(r, carry):
        rows = pl.ds(pl.multiple_of(r * ROW_CHUNK, ROW_CHUNK), ROW_CHUNK)
        x = x_ref[rows, :]
        y = x * lax.rsqrt(jnp.mean(x * x, axis=-1, keepdims=True) + EPS) * nw_ref[...]
        h_ref[rows, :] = (y * (1.0 + scale) + shift).astype(BF16)
        return carry

    lax.fori_loop(0, n, body, 0)


def _inproj_kernel(x_ref, mod_ref, nw_ref, w_ref, o_ref, h_ref):
    @pl.when(pl.program_id(1) == 0)
    def _():
        _norm_mod_to(h_ref, x_ref, nw_ref, mod_ref[0, 0:1, :], mod_ref[0, 1:2, :])

    o_ref[...] = jnp.dot(h_ref[...], w_ref[...], preferred_element_type=F32)


def _inproj(x, mod, nw, w):
    tm, tn = 1024, 1024
    return pl.pallas_call(
        _inproj_kernel,
        grid=(N_TOK // tm, N_P // tn),
        in_specs=[pl.BlockSpec((tm, D_MODEL), lambda i, j: (i, 0)),
                  pl.BlockSpec((1, 6, D_MODEL), lambda i, j: (_mod_row(i, tm), 0, 0)),
                  pl.BlockSpec((1, D_MODEL), lambda i, j: (0, 0)),
                  pl.BlockSpec((D_MODEL, tn), lambda i, j: (0, j))],
        out_specs=pl.BlockSpec((tm, tn), lambda i, j: (i, j)),
        out_shape=jax.ShapeDtypeStruct((N_TOK, N_P), F32),
        scratch_shapes=[pltpu.VMEM((tm, D_MODEL), BF16)],
        compiler_params=pltpu.CompilerParams(dimension_semantics=("arbitrary", "arbitrary"),
                                             vmem_limit_bytes=VMEM_LIMIT),
        name="inproj",
    )(x, mod, nw, w)


MERGE_TM = 256
MERGE_CTX_TILES = N_CTX_TOK // MERGE_TM


def _merge_kernel(x_ref, *refs):
    y_refs, (g_ref, mod_ref, wb_ref, wo_ref, o_ref) = refs[:2 * N_BRANCH], refs[2 * N_BRANCH:]
    is_ctx = pl.program_id(0) < MERGE_CTX_TILES
    merged = None
    for k in range(N_BRANCH):
        y = jnp.where(is_ctx, y_refs[2 * k][...], y_refs[2 * k + 1][...])
        yp = jnp.dot(y.astype(BF16), wb_ref[k], preferred_element_type=F32)
        t = jax.nn.sigmoid(g_ref[:, k * D_MODEL:(k + 1) * D_MODEL]) * yp
        merged = t if merged is None else merged + t
    o = jnp.dot(merged.astype(BF16), wo_ref[...], preferred_element_type=F32)
    o_ref[...] = x_ref[...] + mod_ref[0, 2:3, :] * o


def _merge(x, ys, p, mod, wb, wo):
    tm = MERGE_TM
    nct = MERGE_CTX_TILES
    cspec = pl.BlockSpec((tm, BRANCH_WIDTH), lambda i: (jnp.minimum(i, nct - 1), 0))
    lspec = pl.BlockSpec((tm, BRANCH_WIDTH), lambda i: (jnp.maximum(i - nct, 0), 0))
    return pl.pallas_call(
        _merge_kernel,
        grid=(N_TOK // tm,),
        in_specs=[pl.BlockSpec((tm, D_MODEL), lambda i: (i, 0))]
        + [cspec, lspec] * N_BRANCH
        + [pl.BlockSpec((tm, N_BRANCH * D_MODEL), lambda i: (i, 0)),
           pl.BlockSpec((1, 6, D_MODEL), lambda i: (_mod_row(i, tm), 0, 0)),
           pl.BlockSpec((N_BRANCH, BRANCH_WIDTH, D_MODEL), lambda i: (0, 0, 0)),
           pl.BlockSpec((D_MODEL, D_MODEL), lambda i: (0, 0))],
        out_specs=pl.BlockSpec((tm, D_MODEL), lambda i: (i, 0)),
        out_shape=jax.ShapeDtypeStruct((N_TOK, D_MODEL), F32),
        compiler_params=pltpu.CompilerParams(dimension_semantics=("arbitrary",),
                                             vmem_limit_bytes=VMEM_LIMIT),
        name="merge",
    )(x, *[y for pair in ys for y in pair], p, mod, wb, wo)


def _mlp_kernel(x_ref, mod_ref, nw_ref, w1_ref, w2_ref, fw_ref, o_ref, h_ref, acc_ref, *, final):
    f = pl.program_id(1)

    @pl.when(f == 0)
    def _():
        _norm_mod_to(h_ref, x_ref, nw_ref, mod_ref[0, 3:4, :], mod_ref[0, 4:5, :])

    a = jnp.dot(h_ref[...], w1_ref[...], preferred_element_type=F32)
    a = jnp.square(jnp.maximum(a, 0.0)).astype(BF16)
    contrib = jnp.dot(a, w2_ref[...], preferred_element_type=F32)

    @pl.when(f == 0)
    def _():
        acc_ref[...] = contrib

    @pl.when(f != 0)
    def _():
        acc_ref[...] += contrib

    @pl.when(f == pl.num_programs(1) - 1)
    def _():
        y = x_ref[...] + mod_ref[0, 5:6, :] * acc_ref[...]
        if final:
            y = y * lax.rsqrt(jnp.mean(y * y, axis=-1, keepdims=True) + EPS) * fw_ref[...]
        o_ref[...] = y


def _mlp(x, mod, nw, w1, w2, fw, final):
    tm, tf = 1024, 512
    return pl.pallas_call(
        functools.partial(_mlp_kernel, final=final),
        grid=(N_TOK // tm, D_FF // tf),
        in_specs=[pl.BlockSpec((tm, D_MODEL), lambda i, f: (i, 0)),
                  pl.BlockSpec((1, 6, D_MODEL), lambda i, f: (_mod_row(i, tm), 0, 0)),
                  pl.BlockSpec((1, D_MODEL), lambda i, f: (0, 0)),
                  pl.BlockSpec((D_MODEL, tf), lambda i, f: (0, f)),
                  pl.BlockSpec((tf, D_MODEL), lambda i, f: (f, 0)),
                  pl.BlockSpec((1, D_MODEL), lambda i, f: (0, 0))],
        out_specs=pl.BlockSpec((tm, D_MODEL), lambda i, f: (i, 0)),
        out_shape=jax.ShapeDtypeStruct((N_TOK, D_MODEL), F32),
        scratch_shapes=[pltpu.VMEM((tm, D_MODEL), BF16), pltpu.VMEM((tm, D_MODEL), F32)],
        compiler_params=pltpu.CompilerParams(dimension_semantics=("arbitrary", "arbitrary"),
                                             vmem_limit_bytes=VMEM_LIMIT),
        name="mlp",
    )(x, mod, nw, w1, w2, fw)


HD = HEAD_DIM
N_HD = 2 * B_HEADS
CONV_PAD = 8


def _dot_nt(a, b):
    return lax.dot_general(a, b, (((1,), (1,)), ((), ())), preferred_element_type=F32)


def _dot_tn(a, b):
    return lax.dot_general(a, b, (((0,), (0,)), ((), ())), preferred_element_type=F32)


def _dot(a, b):
    return jnp.dot(a, b, preferred_element_type=F32)


def _split(x):
    hi = x.astype(BF16)
    return hi, (x - hi.astype(F32)).astype(BF16)


def _dot3(a, b):
    return _dot(a[0], b[0]) + (_dot(a[0], b[1]) + _dot(a[1], b[0]))


def _softplus(x):
    return jnp.maximum(x, 0.0) + jnp.log1p(jnp.exp(-jnp.abs(x)))


def _delta_kernel(*refs, L, has_s0):
    if has_s0:
        (q_ref, k_ref, v_ref, z_ref, ab_ref, cw_ref, prm_ref, nw_ref, s0_ref, y_ref,
         xpad, qkv_s, g_s, b_s, o_s, st_s) = refs
        sout_ref = None
    else:
        (q_ref, k_ref, v_ref, z_ref, ab_ref, cw_ref, prm_ref, nw_ref, y_ref, sout_ref,
         xpad, qkv_s, g_s, b_s, o_s, st_s) = refs
        s0_ref = None
    C = DELTA_CHUNK
    n_chunks = L // C

    zeros_pad = jnp.zeros((CONV_PAD, BRANCH_WIDTH), F32)
    xpad[0:CONV_PAD, :] = zeros_pad
    xpad[CONV_PAD + L:2 * CONV_PAD + L, :] = zeros_pad
    for idx, src in enumerate((q_ref, k_ref, v_ref)):
        xpad[CONV_PAD:CONV_PAD + L, :] = src[...]
        for r in range(n_chunks):
            acc = None
            for j in range(CONV_K):
                start = CONV_PAD + r * C + j - CONV_K // 2
                t = xpad[start:start + C, :] * cw_ref[j:j + 1, idx * BRANCH_WIDTH:(idx + 1) * BRANCH_WIDTH]
                acc = t if acc is None else acc + t
            y = acc * jax.nn.sigmoid(acc)
            if idx == 2:
                qkv_s[idx, r * C:(r + 1) * C, :] = y
            else:
                for h in range(B_HEADS):
                    yh = y[:, h * HD:(h + 1) * HD]
                    yh = yh * lax.rsqrt(jnp.sum(yh * yh, axis=-1, keepdims=True) + EPS)
                    if idx == 0:
                        yh = yh * ATTN_SCALE
                    qkv_s[idx, r * C:(r + 1) * C, h * HD:(h + 1) * HD] = yh

    ab = ab_ref[...]
    g_s[...] = -jnp.exp(prm_ref[0:1, :]) * _softplus(ab + prm_ref[1:2, :])
    b_s[...] = jax.nn.sigmoid(ab)

    if has_s0:
        st_s[...] = s0_ref[0]
    else:
        st_s[...] = jnp.zeros_like(st_s)

    ri = lax.broadcasted_iota(jnp.int32, (C, C), 0)
    ci = lax.broadcasted_iota(jnp.int32, (C, C), 1)
    eye = (ri == ci).astype(F32)
    level_masks = [(ri // 2) == (ci // 2)]
    blk = 2
    while blk < C:
        level_masks.append(((ri // (2 * blk)) == (ci // (2 * blk))) & ((ri // blk) != (ci // blk)))
        blk *= 2

    def body(n, carry):
        P = []
        for d in range(2):
            chunk = n if d == 0 else n_chunks - 1 - n
            rows = pl.ds(pl.multiple_of(chunk * C, C), C)
            incl = (ri >= ci) if d == 0 else (ri <= ci)
            strict = (ri > ci) if d == 0 else (ri < ci)
            g = g_s[rows, :]
            beta = b_s[rows, :]
            gc = jnp.dot(incl.astype(F32), g, precision=lax.Precision.HIGHEST, preferred_element_type=F32)
            gct = gc.T
            tot = gc[C - 1:C, :] if d == 0 else gc[0:1, :]
            eg = jnp.exp(gc)
            ek = jnp.exp(tot - gc)
            etot = jnp.exp(tot)
            for h in range(B_HEADS):
                c = d * B_HEADS + h
                hs = slice(h * HD, (h + 1) * HD)
                P.append(dict(d=d, c=c, hs=hs, rows=rows, incl=incl, strict=strict,
                              qh=qkv_s[0, rows, hs], kh=qkv_s[1, rows, hs], vh=qkv_s[2, rows, hs],
                              bcol=beta[:, N_HD + c:N_HD + c + 1], gcol=gc[:, c:c + 1], grow=gct[c:c + 1, :],
                              egc=eg[:, c:c + 1], ekc=ek[:, c:c + 1], etc=etot[:, c:c + 1]))
        for p in P:
            p["qk"] = _dot_nt(jnp.concatenate([p["qh"], p["kh"]], axis=0), p["kh"])
        for p in P:
            decay = jnp.exp(jnp.where(p["incl"], p["gcol"] - p["grow"], NEG))
            p["pqk"] = p["qk"][:C] * decay
            p["amat"] = jnp.where(p["strict"], p["bcol"] * p["qk"][C:] * decay, 0.0)
            p["rhs"] = jnp.concatenate([p["bcol"] * p["vh"], (p["bcol"] * p["egc"]) * p["kh"]], axis=1)
            p["tinv"] = eye - jnp.where(level_masks[0], p["amat"], 0.0)
        for lm in level_masks[1:]:
            for p in P:
                p["ts"] = _split(p["tinv"])
                p["et"] = _dot3(_split(jnp.where(lm, p["amat"], 0.0)), p["ts"])
            for p in P:
                p["tinv"] = p["tinv"] - _dot3(p["ts"], _split(p["et"]))
        for p in P:
            p["sol"] = _dot3(_split(p["tinv"]), _split(p["rhs"]))
        for p in P:
            p["s"] = st_s[p["c"]]
            p["t"] = _dot(jnp.concatenate([p["qh"] * p["egc"], p["sol"][:, HD:]], axis=0), p["s"])
        for p in P:
            p["u"] = p["sol"][:, :HD] - p["t"][C:]
            p["o"] = p["t"][:C] + _dot(p["pqk"], p["u"])
            p["s_new"] = p["etc"] * p["s"] + _dot_tn(p["kh"] * p["ekc"], p["u"])
        for p in P:
            o_s[p["d"], p["rows"], p["hs"]] = p["o"]
            st_s[p["c"]] = p["s_new"]
        return carry

    lax.fori_loop(0, n_chunks, body, 0)

    if not has_s0:
        sout_ref[0] = st_s[...]

    def out_body(r, carry):
        rows = pl.ds(pl.multiple_of(r * C, C), C)
        z = z_ref[rows, :]
        gate = z * jax.nn.sigmoid(z) * nw_ref[...]
        for h in range(B_HEADS):
            hs = slice(h * HD, (h + 1) * HD)
            o = o_s[0, rows, hs] + o_s[1, rows, hs]
            o = o * lax.rsqrt(jnp.mean(o * o, axis=-1, keepdims=True) + EPS)
            y_ref[rows, hs] = o * gate[:, hs]
        return carry

    lax.fori_loop(0, n_chunks, out_body, 0)


def _delta(p, cw, prm, nw, s0, *, L, n_seq, row_block0):
    has_s0 = s0 is not None

    def pspec(name):
        off, w = P_OFF[name]
        return pl.BlockSpec((L, w if w >= LANE else LANE), lambda i: (row_block0 + i, off // max(w, LANE)))

    in_specs = [pspec("b_q"), pspec("b_k"), pspec("b_v"), pspec("b_z"), pspec("b_ab"),
                pl.BlockSpec((CONV_K, 3 * BRANCH_WIDTH), lambda i: (0, 0)),
                pl.BlockSpec((8, LANE), lambda i: (0, 0)),
                pl.BlockSpec((1, BRANCH_WIDTH), lambda i: (0, 0))]
    args = [p, p, p, p, p, cw, prm, nw]
    st_spec = pl.BlockSpec((1, N_HD, HD, HD), lambda i: (i, 0, 0, 0))
    y_shape = jax.ShapeDtypeStruct((n_seq * L, BRANCH_WIDTH), F32)
    y_spec = pl.BlockSpec((L, BRANCH_WIDTH), lambda i: (i, 0))
    if has_s0:
        in_specs.append(st_spec)
        args.append(s0)
        out_specs, out_shape = y_spec, y_shape
    else:
        out_specs = (y_spec, st_spec)
        out_shape = (y_shape, jax.ShapeDtypeStruct((n_seq, N_HD, HD, HD), F32))
    return pl.pallas_call(
        functools.partial(_delta_kernel, L=L, has_s0=has_s0),
        grid=(n_seq,),
        in_specs=in_specs,
        out_specs=out_specs,
        out_shape=out_shape,
        scratch_shapes=[pltpu.VMEM((L + 2 * CONV_PAD, BRANCH_WIDTH), F32),
                        pltpu.VMEM((3, L, BRANCH_WIDTH), F32),
                        pltpu.VMEM((L, LANE), F32),
                        pltpu.VMEM((L, LANE), F32),
                        pltpu.VMEM((2, L, BRANCH_WIDTH), F32),
                        pltpu.VMEM((N_HD, HD, HD), F32)],
        compiler_params=pltpu.CompilerParams(dimension_semantics=("arbitrary",),
                                             vmem_limit_bytes=VMEM_LIMIT),
        name="delta_lat" if has_s0 else "delta_ctx",
    )(*args)


def _delta_params(a_log, dt_bias, norm_w):
    prm = jnp.zeros((8, LANE), F32)
    prm = prm.at[0, :N_HD].set(a_log.reshape(N_HD).astype(F32))
    prm = prm.at[1, :N_HD].set(dt_bias.reshape(N_HD).astype(F32))
    return prm, jnp.tile(norm_w.astype(F32), B_HEADS).reshape(1, BRANCH_WIDTH)


def _attend(problems):
    for p in problems:
        p["s"] = [(_dot_nt(p["q"], k) if b is None else _dot_nt(p["q"], k) + b)
                  for k, b in zip(p["ks"], p["biases"])]
    for p in problems:
        m = None
        for s in p["s"]:
            mi = jnp.max(s, axis=-1, keepdims=True)
            m = mi if m is None else jnp.maximum(m, mi)
        if p["sink"] is not None:
            m = jnp.maximum(m, p["sink"])
        p["p"] = [jnp.exp(s - m) for s in p["s"]]
        den = None
        for pr in p["p"]:
            di = jnp.sum(pr, axis=-1, keepdims=True)
            den = di if den is None else den + di
        if p["sink"] is not None:
            den = den + jnp.exp(p["sink"] - m)
        p["den"] = den
    outs = []
    for p in problems:
        acc = None
        for pr, v in zip(p["p"], p["vs"]):
            ai = _dot(pr, v)
            acc = ai if acc is None else acc + ai
        outs.append(acc / p["den"])
    return outs


def _sink_col(sink_ref, heads, rows):
    return jnp.concatenate([jnp.full((rows, 1), sink_ref[h], F32) for h in heads], axis=0)


def _ctx_attn_kernel(sink_ref, aq_ref, ak_ref, av_ref, dq_ref, dk_ref, dv_ref, ya_ref, yd_ref):
    L = aq_ref.shape[0]
    problems = []
    for j in range(A_KV_HEADS):
        js = slice(j * HD, (j + 1) * HD)
        heads = range(j * A_GROUP, (j + 1) * A_GROUP)
        q = jnp.concatenate([aq_ref[:, h * HD:(h + 1) * HD] for h in heads], axis=0) * ATTN_SCALE
        problems.append(dict(q=q, ks=[ak_ref[:, js]], vs=[av_ref[:, js]], biases=[None],
                             sink=_sink_col(sink_ref, heads, L)))
    for h in range(D_HEADS):
        hs = slice(h * HD, (h + 1) * HD)
        problems.append(dict(q=dq_ref[:, hs] * ATTN_SCALE, ks=[dk_ref[:, hs]], vs=[dv_ref[:, hs]],
                             biases=[None], sink=None))
    outs = _attend(problems)
    for j in range(A_KV_HEADS):
        for g in range(A_GROUP):
            h = j * A_GROUP + g
            ya_ref[:, h * HD:(h + 1) * HD] = outs[j][g * L:(g + 1) * L]
    for h in range(D_HEADS):
        yd_ref[:, h * HD:(h + 1) * HD] = outs[A_KV_HEADS + h]


def _pspec(name, rows, row_block0):
    off, w = P_OFF[name]
    bw = max(w, LANE)
    return pl.BlockSpec((rows, bw), lambda i: (row_block0 + i, off // bw))


_SMEM_SPEC = pl.BlockSpec(memory_space=pltpu.SMEM)


def _ctx_attn(p, sink):
    y_shape = jax.ShapeDtypeStruct((N_CTX_TOK, BRANCH_WIDTH), F32)
    y_spec = pl.BlockSpec((SEQ, BRANCH_WIDTH), lambda i: (i, 0))
    return pl.pallas_call(
        _ctx_attn_kernel,
        grid=(BATCH,),
        in_specs=[_SMEM_SPEC] + [_pspec(n, SEQ, 0) for n in ("a_q", "a_k", "a_v", "d_q", "d_k", "d_v")],
        out_specs=(y_spec, y_spec),
        out_shape=(y_shape, y_shape),
        compiler_params=pltpu.CompilerParams(dimension_semantics=("arbitrary",),
                                             vmem_limit_bytes=VMEM_LIMIT),
        name="ctx_attn",
    )(sink, p, p, p, p, p, p)


def _rope_tables():
    t = np.arange(DEC_SEQ)
    quarter = HD // 4
    inv = ROPE_BASE ** (-np.arange(quarter, dtype=np.float64) / quarter)
    ang_r = (t // GRID_W)[:, None] * inv[None, :]
    ang_c = (t % GRID_W)[:, None] * inv[None, :]
    cos = np.concatenate([np.cos(ang_r)] * 2 + [np.cos(ang_c)] * 2, axis=1)
    sin = np.concatenate([-np.sin(ang_r), np.sin(ang_r), -np.sin(ang_c), np.sin(ang_c)], axis=1)
    return (jnp.asarray(np.tile(cos, (1, 2)), F32), jnp.asarray(np.tile(sin, (1, 2)), F32))


def _rope128(x, cos, sin):
    lane = lax.broadcasted_iota(jnp.int32, x.shape, 1)
    swapped = jnp.where((lane % 32) < 16, pltpu.roll(x, LANE - 16, 1), pltpu.roll(x, 16, 1))
    return x * cos + swapped * sin


def _win_attn_kernel(sink_ref, q_ref, k_ref, v_ref, ck_ref, cv_ref, cos_ref, sin_ref, y_ref, qr_s, kr_s):
    L = DEC_SEQ
    nb = L // A_BLOCK
    cos = cos_ref[...]
    sin = sin_ref[...]
    kr_s[...] = _rope128(k_ref[...], cos, sin)
    for c in range(BRANCH_WIDTH // LANE):
        cs = slice(c * LANE, (c + 1) * LANE)
        qr_s[:, cs] = _rope128(q_ref[:, cs], cos, sin) * ATTN_SCALE

    W = 3 * A_BLOCK
    qi = lax.broadcasted_iota(jnp.int32, (A_BLOCK, W), 0)
    ki = lax.broadcasted_iota(jnp.int32, (A_BLOCK, W), 1)

    def body(i, carry):
        start = jnp.clip(i - 1, 0, nb - 3) * A_BLOCK
        rows = pl.ds(pl.multiple_of(i * A_BLOCK, A_BLOCK), A_BLOCK)
        krows = pl.ds(pl.multiple_of(start, A_BLOCK), W)
        ok = jnp.abs(i * A_BLOCK + qi - (start + ki)) <= A_WINDOW
        bias = jnp.where(ok, 0.0, NEG)
        bias4 = jnp.concatenate([bias] * A_GROUP, axis=0)
        problems = []
        for j in range(A_KV_HEADS):
            js = slice(j * HD, (j + 1) * HD)
            heads = range(j * A_GROUP, (j + 1) * A_GROUP)
            q = jnp.concatenate([qr_s[rows, h * HD:(h + 1) * HD] for h in heads], axis=0)
            problems.append(dict(q=q, ks=[kr_s[krows, js], ck_ref[0, 0, :, js]],
                                 vs=[v_ref[krows, js], cv_ref[0, 0, :, js]], biases=[bias4, None],
                                 sink=_sink_col(sink_ref, heads, A_BLOCK)))
        outs = _attend(problems)
        for j in range(A_KV_HEADS):
            for g in range(A_GROUP):
                h = j * A_GROUP + g
                y_ref[rows, h * HD:(h + 1) * HD] = outs[j][g * A_BLOCK:(g + 1) * A_BLOCK]
        return carry

    lax.fori_loop(0, nb, body, 0)


def _win_attn(p, sink, cache_k, cache_v, layer, cos, sin):
    row0 = N_CTX_TOK // DEC_SEQ
    cspec = pl.BlockSpec((1, 1, PAST_LEN, LANE), lambda i: (i, layer, 0, 0))
    tspec = pl.BlockSpec((DEC_SEQ, LANE), lambda i: (0, 0))
    return pl.pallas_call(
        _win_attn_kernel,
        grid=(DEC_BATCH,),
        in_specs=[_SMEM_SPEC] + [_pspec(n, DEC_SEQ, row0) for n in ("a_q", "a_k", "a_v")]
        + [cspec, cspec, tspec, tspec],
        out_specs=pl.BlockSpec((DEC_SEQ, BRANCH_WIDTH), lambda i: (i, 0)),
        out_shape=jax.ShapeDtypeStruct((N_LAT_TOK, BRANCH_WIDTH), F32),
        scratch_shapes=[pltpu.VMEM((DEC_SEQ, BRANCH_WIDTH), F32), pltpu.VMEM((DEC_SEQ, LANE), F32)],
        compiler_params=pltpu.CompilerParams(dimension_semantics=("arbitrary",),
                                             vmem_limit_bytes=VMEM_LIMIT),
        name="win_attn",
    )(sink, p, p, p, cache_k, cache_v, cos, sin)


N_GRID_ROWS = DEC_SEQ // GRID_W
N_DR = 2 * NH_ROWS - 1


def _na_bias_table(rpb):
    qc = np.arange(GRID_W)[:, None]
    kc = np.arange(GRID_W)[None, :]
    wstart = np.clip(qc - NH_COLS // 2, 0, GRID_W - NH_COLS)
    ok = (kc >= wstart) & (kc < wstart + NH_COLS)
    dc = np.clip(kc - qc + NH_COLS - 1, 0, 2 * NH_COLS - 2)
    t = jnp.where(ok[None, None], rpb.astype(F32)[:, :, dc], NEG)
    return jnp.concatenate([t[:, :-1], t[:, 1:]], axis=-1)


def _na_attn_kernel(q_ref, k_ref, v_ref, ck_ref, cv_ref, t_ref, y_ref):
    kh = NH_ROWS
    n_loc = kh * GRID_W

    def body(r, carry):
        rs = jnp.clip(r - kh // 2, 0, N_GRID_ROWS - kh)
        rows = pl.ds(pl.multiple_of(r * GRID_W, GRID_W), GRID_W)
        krows = pl.ds(pl.multiple_of(rs * GRID_W, GRID_W), n_loc)
        s0 = rs - r + NH_ROWS - 1
        problems = []
        for h in range(D_HEADS):
            hs = slice(h * HD, (h + 1) * HD)
            bias = jnp.concatenate([t_ref[h, s0 + 2 * w] for w in range(kh // 2)], axis=1)
            problems.append(dict(q=q_ref[rows, hs] * ATTN_SCALE,
                                 ks=[k_ref[krows, hs], ck_ref[0, 0, :, hs]],
                                 vs=[v_ref[krows, hs], cv_ref[0, 0, :, hs]], biases=[bias, None], sink=None))
        outs = _attend(problems)
        for h in range(D_HEADS):
            y_ref[rows, h * HD:(h + 1) * HD] = outs[h]
        return carry

    lax.fori_loop(0, N_GRID_ROWS, body, 0)


def _na_attn(p, cache_k, cache_v, layer, table):
    row0 = N_CTX_TOK // DEC_SEQ
    cspec = pl.BlockSpec((1, 1, PAST_LEN, BRANCH_WIDTH), lambda i: (i, layer, 0, 0))
    return pl.pallas_call(
        _na_attn_kernel,
        grid=(DEC_BATCH,),
        in_specs=[_pspec(n, DEC_SEQ, row0) for n in ("d_q", "d_k", "d_v")]
        + [cspec, cspec, pl.BlockSpec((D_HEADS, N_DR - 1, GRID_W, LANE), lambda i: (0, 0, 0, 0))],
        out_specs=pl.BlockSpec((DEC_SEQ, BRANCH_WIDTH), lambda i: (i, 0)),
        out_shape=jax.ShapeDtypeStruct((N_LAT_TOK, BRANCH_WIDTH), F32),
        compiler_params=pltpu.CompilerParams(dimension_semantics=("arbitrary",),
                                             vmem_limit_bytes=VMEM_LIMIT),
        name="na_attn",
    )(p, p, p, cache_k, cache_v, table)


HG = 128
N_PAIR = C_HEADS // 2
CPG = HG // HGRN_CHUNK


def _hgrn_sel():
    j = np.arange(HGRN_CHUNK)[:, None, None, None]
    lane = np.arange(LANE)[None, :, None, None]
    e = np.arange(2)[None, None, :, None]
    c = np.arange(LANE)[None, None, None, :]
    sel = ((lane // HD) == e) & ((c % HGRN_CHUNK) == j)
    return jnp.asarray(sel.reshape(HGRN_CHUNK * LANE, 2 * LANE), BF16)


def _hgrn_kernel(*refs, L, has_s0):
    if has_s0:
        (q_ref, f_ref, i_ref, g_ref, lb_ref, nw_ref, sel_ref, s0_ref, y_ref,
         lf_s, ck_s, qs_s, o_s, st_s, zc_s) = refs
        sout_ref = None
    else:
        (q_ref, f_ref, i_ref, g_ref, lb_ref, nw_ref, sel_ref, y_ref, sout_ref,
         lf_s, ck_s, qs_s, o_s, st_s, zc_s) = refs
        s0_ref = None
    C = HGRN_CHUNK
    n_groups = L // HG
    R = 64

    lb = lb_ref[...]
    log_lb = jnp.log(lb)
    log_1mlb = jnp.log1p(-lb)

    def pre_body(r, carry):
        rows = pl.ds(pl.multiple_of(r * R, R), R)
        cf = f_ref[rows, :]
        b = log_1mlb - _softplus(-cf)
        lf_s[rows, :] = jnp.maximum(log_lb, b) + jnp.log1p(jnp.exp(-jnp.abs(log_lb - b)))
        ck_s[rows, :] = (1.0 - lb) * jax.nn.sigmoid(-cf)
        cq = q_ref[rows, :]
        qs_s[rows, :] = cq * jax.nn.sigmoid(cq)
        return carry

    lax.fori_loop(0, L // R, pre_body, 0)

    if has_s0:
        st_s[...] = s0_ref[0]
    else:
        st_s[...] = jnp.zeros_like(st_s)

    ri = lax.broadcasted_iota(jnp.int32, (HG, HG), 0)
    ci = lax.broadcasted_iota(jnp.int32, (HG, HG), 1)
    same_chunk = (ri // C) == (ci // C)
    same_head = (ri // HD) == (ci // HD)
    tl = ri % C

    cum_mats = [(same_chunk & ((ci <= ri) if d == 0 else (ci >= ri))).astype(F32) for d in range(2)]

    def body(n, carry):
        P = []
        for d in range(2):
            gi = n if d == 0 else n_groups - 1 - n
            rows = pl.ds(pl.multiple_of(gi * HG, HG), HG)
            for hp in range(N_PAIR):
                fcols = slice(d * BRANCH_WIDTH + hp * LANE, d * BRANCH_WIDTH + (hp + 1) * LANE)
                hcols = slice(hp * LANE, (hp + 1) * LANE)
                P.append(dict(d=d, sidx=d * N_PAIR + hp, rows=rows, hcols=hcols,
                              lf=lf_s[rows, fcols], kk=ck_s[rows, fcols], qq=qs_s[rows, hcols],
                              vv=i_ref[rows, hcols]))
        for p in P:
            p["bcum"] = jnp.dot(cum_mats[p["d"]], p["lf"], precision=lax.Precision.HIGHEST,
                                preferred_element_type=F32)
        for p in P:
            b3 = p["bcum"].reshape(CPG, C, LANE)
            k3 = p["kk"].reshape(CPG, C, LANE)
            for j in range(C):
                bj = jnp.broadcast_to(b3[:, j:j + 1, :], (CPG, C, LANE)).reshape(HG, LANE)
                kj = jnp.broadcast_to(k3[:, j:j + 1, :], (CPG, C, LANE)).reshape(HG, LANE)
                ok = (tl >= j) if p["d"] == 0 else (tl <= j)
                z = p["qq"] * jnp.exp(jnp.where(ok, p["bcum"] - bj, NEG)) * kj
                zc_s[p["sidx"], :, j * LANE:(j + 1) * LANE] = z.astype(BF16)
        for p in P:
            p["att"] = _dot(zc_s[p["sidx"]], sel_ref[...])
        for p in P:
            p["o_intra"] = jnp.concatenate(
                [_dot(jnp.where(same_chunk, p["att"][:, e * LANE:(e + 1) * LANE], 0.0),
                      p["vv"][:, e * HD:(e + 1) * HD]) for e in range(2)], axis=1)
            p["st"] = st_s[p["sidx"]]
            p["o_inter"] = [None] * CPG
        for cix in range(CPG):
            for p in P:
                c = cix if p["d"] == 0 else CPG - 1 - cix
                r16 = slice(c * C, (c + 1) * C)
                bc = p["bcum"][r16]
                blast = bc[C - 1:C] if p["d"] == 0 else bc[0:1]
                p["o_inter"][c] = _dot_nt(p["qq"][r16] * jnp.exp(bc), p["st"])
                upd = _dot_tn(p["vv"][r16], p["kk"][r16] * jnp.exp(blast - bc))
                p["st"] = p["st"] * jnp.exp(blast) + jnp.where(same_head, upd, 0.0)
        for p in P:
            o_s[p["d"], p["rows"], p["hcols"]] = p["o_intra"] + jnp.concatenate(p["o_inter"], axis=0)
            st_s[p["sidx"]] = p["st"]
        return carry

    lax.fori_loop(0, n_groups, body, 0)

    if not has_s0:
        sout_ref[0] = st_s[...]

    def out_body(r, carry):
        rows = pl.ds(pl.multiple_of(r * R, R), R)
        gate = jax.nn.sigmoid(g_ref[rows, :])
        for h in range(C_HEADS):
            hs = slice(h * HD, (h + 1) * HD)
            o = (o_s[0, rows, hs] + o_s[1, rows, hs]) * gate[:, hs]
            y_ref[rows, hs] = o * lax.rsqrt(jnp.mean(o * o, axis=-1, keepdims=True) + EPS) * nw_ref[:, hs]
        return carry

    lax.fori_loop(0, L // R, out_body, 0)


def _hgrn(p, lb, nw, sel, s0, *, L, n_seq, row_block0):
    has_s0 = s0 is not None
    in_specs = [_pspec(n, L, row_block0) for n in ("c_q", "c_f", "c_i", "c_g")] + [
        pl.BlockSpec((1, 2 * BRANCH_WIDTH), lambda i: (0, 0)),
        pl.BlockSpec((1, BRANCH_WIDTH), lambda i: (0, 0)),
        pl.BlockSpec((HGRN_CHUNK * LANE, 2 * LANE), lambda i: (0, 0))]
    args = [p, p, p, p, lb, nw, sel]
    st_spec = pl.BlockSpec((1, 2 * N_PAIR, LANE, LANE), lambda i: (i, 0, 0, 0))
    y_shape = jax.ShapeDtypeStruct((n_seq * L, BRANCH_WIDTH), F32)
    y_spec = pl.BlockSpec((L, BRANCH_WIDTH), lambda i: (i, 0))
    if has_s0:
        in_specs.append(st_spec)
        args.append(s0)
        out_specs, out_shape = y_spec, y_shape
    else:
        out_specs = (y_spec, st_spec)
        out_shape = (y_shape, jax.ShapeDtypeStruct((n_seq, 2 * N_PAIR, LANE, LANE), F32))
    return pl.pallas_call(
        functools.partial(_hgrn_kernel, L=L, has_s0=has_s0),
        grid=(n_seq,),
        in_specs=in_specs,
        out_specs=out_specs,
        out_shape=out_shape,
        scratch_shapes=[pltpu.VMEM((L, 2 * BRANCH_WIDTH), F32),
                        pltpu.VMEM((L, 2 * BRANCH_WIDTH), F32),
                        pltpu.VMEM((L, BRANCH_WIDTH), F32),
                        pltpu.VMEM((2, L, BRANCH_WIDTH), F32),
                        pltpu.VMEM((2 * N_PAIR, LANE, LANE), F32),
                        pltpu.VMEM((2 * N_PAIR, HG, HGRN_CHUNK * LANE), BF16)],
        compiler_params=pltpu.CompilerParams(dimension_semantics=("arbitrary",),
                                             vmem_limit_bytes=VMEM_LIMIT),
        name="hgrn_lat" if has_s0 else "hgrn_ctx",
    )(*args)


def _hgrn_state_in(s):
    b = s.shape[0]
    st = jnp.swapaxes(s.astype(F32), -1, -2).reshape(b, 2, N_PAIR, 2, HD, HD)
    z = jnp.zeros_like(st[:, :, :, 0])
    top = jnp.concatenate([st[:, :, :, 0], z], axis=-1)
    bot = jnp.concatenate([z, st[:, :, :, 1]], axis=-1)
    return jnp.concatenate([top, bot], axis=-2).reshape(b, 2 * N_PAIR, LANE, LANE)


def _hgrn_state_out(s):
    b = s.shape[0]
    s = s.reshape(b, 2, N_PAIR, LANE, LANE)
    blocks = jnp.stack([s[..., :HD, :HD], s[..., HD:, HD:]], axis=3)
    return jnp.swapaxes(blocks, -1, -2).reshape(b, 2, C_HEADS, HD, HD)


def _pcol(p, name):
    o, w = P_OFF[name]
    return p[..., o:o + w]


def _pack_w_in(w_in):
    parts = []
    used = 0
    for name in _P_ORDER:
        o, w = REF_OFF[name]
        parts.append(w_in[..., o:o + w])
        pw = -(-w // LANE) * LANE
        if pw != w:
            parts.append(jnp.zeros(w_in.shape[:-1] + (pw - w,), w_in.dtype))
        used += pw
    parts.append(jnp.zeros(w_in.shape[:-1] + (N_P - used,), w_in.dtype))
    return jnp.concatenate(parts, axis=-1).astype(BF16)


def kernel(x_prompt, x_sample, cache_attn_k, cache_attn_v, cache_na_k, cache_na_v, state_delta, state_hgrn,
           c, c_ctx, norm_w, ada_w, ada_b, w_in, attn_sink, delta_conv, delta_a_log, delta_dt_bias,
           delta_norm_w, hgrn_lb, hgrn_norm_w, na_rpb, w_branch, w_out, mlp_w1, mlp_w2, final_norm_w):
    lb = jnp.cumsum(jax.nn.softmax(hgrn_lb.astype(F32), axis=0), axis=0)
    lb = lb - lb[:1]

    cvec = jnp.concatenate([c_ctx[None, :], c, jnp.zeros((16 - N_MOD_ROWS, D_MODEL), F32)], axis=0)
    mod = _adaln(cvec, ada_w, ada_b).reshape(DEPTH, 16, 6, D_MODEL)

    w_in_p = _pack_w_in(w_in)
    wb = w_branch.astype(BF16)
    wo = w_out.astype(BF16)
    w1 = mlp_w1.astype(BF16)
    w2 = mlp_w2.astype(BF16)
    fw = final_norm_w.reshape(1, D_MODEL)

    cos, sin = _rope_tables()
    sel = _hgrn_sel()
    cak = cache_attn_k.reshape(DEC_BATCH, DEPTH, PAST_LEN, A_KV_HEADS * HD)
    cav = cache_attn_v.reshape(DEC_BATCH, DEPTH, PAST_LEN, A_KV_HEADS * HD)
    cnk = cache_na_k.reshape(DEC_BATCH, DEPTH, PAST_LEN, BRANCH_WIDTH)
    cnv = cache_na_v.reshape(DEC_BATCH, DEPTH, PAST_LEN, BRANCH_WIDTH)
    lat_blk0 = N_CTX_TOK // DEC_SEQ

    x = jnp.concatenate([x_prompt.reshape(N_CTX_TOK, D_MODEL), x_sample.reshape(N_LAT_TOK, D_MODEL)], axis=0)
    ak_l, av_l, nk_l, nv_l, sd_l, sh_l = [], [], [], [], [], []
    for l in range(DEPTH):
        p = _inproj(x, mod[l], norm_w[l, 0].reshape(1, D_MODEL), w_in_p[l])
        pc = p[:N_CTX_TOK]
        ak_l.append(_pcol(pc, "a_k").reshape(BATCH, SEQ, A_KV_HEADS, HD))
        av_l.append(_pcol(pc, "a_v").reshape(BATCH, SEQ, A_KV_HEADS, HD))
        nk_l.append(_pcol(pc, "d_k").reshape(BATCH, SEQ, D_HEADS, HD))
        nv_l.append(_pcol(pc, "d_v").reshape(BATCH, SEQ, D_HEADS, HD))

        ya_c, yd_c = _ctx_attn(p, attn_sink[l])
        ya_l = _win_attn(p, attn_sink[l], cak, cav, l, cos, sin)
        yd_l = _na_attn(p, cnk, cnv, l, _na_bias_table(na_rpb[l]))

        prm, dnw = _delta_params(delta_a_log[l], delta_dt_bias[l], delta_norm_w[l])
        yb_c, sd = _delta(p, delta_conv[l], prm, dnw, None, L=SEQ, n_seq=BATCH, row_block0=0)
        yb_l = _delta(p, delta_conv[l], prm, dnw, state_delta[:, l].reshape(DEC_BATCH, N_HD, HD, HD),
                      L=DEC_SEQ, n_seq=DEC_BATCH, row_block0=lat_blk0)
        sd_l.append(sd.reshape(BATCH, 2, B_HEADS, HD, HD))

        lbl = lb[l].reshape(1, 2 * BRANCH_WIDTH)
        hnw = jnp.tile(hgrn_norm_w[l].astype(F32), C_HEADS).reshape(1, BRANCH_WIDTH)
        yc_c, sh = _hgrn(p, lbl, hnw, sel, None, L=SEQ, n_seq=BATCH, row_block0=0)
        yc_l = _hgrn(p, lbl, hnw, sel, _hgrn_state_in(state_hgrn[:, l]),
                     L=DEC_SEQ, n_seq=DEC_BATCH, row_block0=lat_blk0)
        sh_l.append(_hgrn_state_out(sh))

        x = _merge(x, [(ya_c, ya_l), (yb_c, yb_l), (yc_c, yc_l), (yd_c, yd_l)], p, mod[l], wb[l], wo[l])
        x = _mlp(x, mod[l], norm_w[l, 1].reshape(1, D_MODEL), w1[l], w2[l], fw, final=(l == DEPTH - 1))

    y_prompt = x[:N_CTX_TOK].reshape(BATCH, SEQ, D_MODEL)
    y_sample = x[N_CTX_TOK:].reshape(DEC_BATCH, DEC_SEQ, D_MODEL)
    return (y_prompt, y_sample, jnp.stack(ak_l, axis=1), jnp.stack(av_l, axis=1), jnp.stack(nk_l, axis=1),
            jnp.stack(nv_l, axis=1), jnp.stack(sd_l, axis=1), jnp.stack(sh_l, axis=1))
```

```python
import functools
import math

import jax
import jax.numpy as jnp
import numpy as np
from jax import lax
from jax.experimental import pallas as pl
from jax.experimental.pallas import tpu as pltpu

F32 = jnp.float32
BF16 = jnp.bfloat16

D_MODEL = 1024
BATCH = 16
SEQ = 256
DEPTH = 4
DEC_BATCH = 8
DEC_SEQ = 1024
PAST_LEN = 512
GRID_W = 64
HEAD_DIM = 64
BRANCH_WIDTH = 512
N_BRANCH = 4
A_HEADS = 8
A_KV_HEADS = 2
A_GROUP = 4
A_WINDOW = 128
A_BLOCK = 128
Q_BLOCK = 128
B_HEADS = 8
DELTA_CHUNK = 64
CONV_K = 5
C_HEADS = 8
HGRN_CHUNK = 16
D_HEADS = 8
NH_ROWS = 8
NH_COLS = 16
NH_QCOLS = 16
NH_KCOLS = 32
D_FF = 4 * D_MODEL
ROPE_BASE = 10000.0
ATTN_SCALE = HEAD_DIM ** -0.5
EPS = 1e-6
NEG = -1e30

N_CTX_TOK = BATCH * SEQ
N_LAT_TOK = DEC_BATCH * DEC_SEQ
N_TOK = N_CTX_TOK + N_LAT_TOK
N_MOD_ROWS = 1 + DEC_BATCH

_REF_COLS = (("a_q", 512), ("a_k", 128), ("a_v", 128), ("b_q", 512), ("b_k", 512), ("b_v", 512),
             ("b_z", 512), ("b_ab", 32), ("c_q", 512), ("c_f", 1024), ("c_i", 512), ("c_g", 512),
             ("d_q", 512), ("d_k", 512), ("d_v", 512), ("g", 4096))
_P_ORDER = ("g", "a_q", "b_q", "b_k", "b_v", "b_z", "c_q", "c_f", "c_i", "c_g", "d_q", "d_k", "d_v",
            "a_k", "a_v", "b_ab")
LANE = 128
N_P = 11264


def _layout():
    ref_off, o = {}, 0
    for name, w in _REF_COLS:
        ref_off[name] = (o, w)
        o += w
    p_off, o = {}, 0
    for name in _P_ORDER:
        w = ref_off[name][1]
        p_off[name] = (o, w)
        o += -(-w // LANE) * LANE
    assert o <= N_P
    return ref_off, p_off


REF_OFF, P_OFF = _layout()

VMEM_LIMIT = 56 * 1024 * 1024


def _mod_row(i, tm):
    nct = N_CTX_TOK // tm
    tpl = DEC_SEQ // tm
    return jnp.where(i < nct, 0, 1 + (i - nct) // tpl)


def _adaln_kernel(c_ref, w_ref, b_ref, o_ref):
    c = c_ref[...]
    s = c * jax.nn.sigmoid(c)
    o_ref[0] = jnp.dot(s, w_ref[0], preferred_element_type=F32) + b_ref[0]


def _adaln(cvec, ada_w, ada_b):
    tn = 1536
    n6 = 6 * D_MODEL
    rows = cvec.shape[0]
    return pl.pallas_call(
        _adaln_kernel,
        grid=(DEPTH, n6 // tn),
        in_specs=[pl.BlockSpec((rows, D_MODEL), lambda l, j: (0, 0)),
                  pl.BlockSpec((1, D_MODEL, tn), lambda l, j: (l, 0, j)),
                  pl.BlockSpec((1, 1, tn), lambda l, j: (l, 0, j))],
        out_specs=pl.BlockSpec((1, rows, tn), lambda l, j: (l, 0, j)),
        out_shape=jax.ShapeDtypeStruct((DEPTH, rows, n6), F32),
        compiler_params=pltpu.CompilerParams(dimension_semantics=("arbitrary", "arbitrary"),
                                             vmem_limit_bytes=VMEM_LIMIT),
        name="adaln",
    )(cvec, ada_w, ada_b.reshape(DEPTH, 1, n6))


ROW_CHUNK = 128


def _norm_mod_to(h_ref, x_ref, nw_ref, shift, scale):
    n = x_ref.shape[0] // ROW_CHUNK

    def body(r, carry):
        rows = pl.ds(pl.multiple_of(r * ROW_CHUNK, ROW_CHUNK), ROW_CHUNK)
        x = x_ref[rows, :]
        y = x * lax.rsqrt(jnp.mean(x * x, axis=-1, keepdims=True) + EPS) * nw_ref[...]
        h_ref[rows, :] = (y * (1.0 + scale) + shift).astype(BF16)
        return carry

    lax.fori_loop(0, n, body, 0)


def _inproj_kernel(x_ref, mod_ref, nw_ref, w_ref, o_ref, h_ref):
    @pl.when(pl.program_id(1) == 0)
    def _():
        _norm_mod_to(h_ref, x_ref, nw_ref, mod_ref[0, 0:1, :], mod_ref[0, 1:2, :])

    o_ref[...] = jnp.dot(h_ref[...], w_ref[...], preferred_element_type=F32)


def _inproj(x, mod, nw, w):
    tm, tn = 1024, 1024
    return pl.pallas_call(
        _inproj_kernel,
        grid=(N_TOK // tm, N_P // tn),
        in_specs=[pl.BlockSpec((tm, D_MODEL), lambda i, j: (i, 0)),
                  pl.BlockSpec((1, 6, D_MODEL), lambda i, j: (_mod_row(i, tm), 0, 0)),
                  pl.BlockSpec((1, D_MODEL), lambda i, j: (0, 0)),
                  pl.BlockSpec((D_MODEL, tn), lambda i, j: (0, j))],
        out_specs=pl.BlockSpec((tm, tn), lambda i, j: (i, j)),
        out_shape=jax.ShapeDtypeStruct((N_TOK, N_P), F32),
        scratch_shapes=[pltpu.VMEM((tm, D_MODEL), BF16)],
        compiler_params=pltpu.CompilerParams(dimension_semantics=("arbitrary", "arbitrary"),
                                             vmem_limit_bytes=VMEM_LIMIT),
        name="inproj",
    )(x, mod, nw, w)


MERGE_TM = 256
MERGE_CTX_TILES = N_CTX_TOK // MERGE_TM


def _merge_kernel(x_ref, *refs):
    y_refs, (g_ref, mod_ref, wb_ref, wo_ref, o_ref) = refs[:2 * N_BRANCH], refs[2 * N_BRANCH:]
    is_ctx = pl.program_id(0) < MERGE_CTX_TILES
    merged = None
    for k in range(N_BRANCH):
        y = jnp.where(is_ctx, y_refs[2 * k][...], y_refs[2 * k + 1][...])
        yp = jnp.dot(y.astype(BF16), wb_ref[k], preferred_element_type=F32)
        t = jax.nn.sigmoid(g_ref[:, k * D_MODEL:(k + 1) * D_MODEL]) * yp
        merged = t if merged is None else merged + t
    o = jnp.dot(merged.astype(BF16), wo_ref[...], preferred_element_type=F32)
    o_ref[...] = x_ref[...] + mod_ref[0, 2:3, :] * o


def _merge(x, ys, p, mod, wb, wo):
    tm = MERGE_TM
    nct = MERGE_CTX_TILES
    cspec = pl.BlockSpec((tm, BRANCH_WIDTH), lambda i: (jnp.minimum(i, nct - 1), 0))
    lspec = pl.BlockSpec((tm, BRANCH_WIDTH), lambda i: (jnp.maximum(i - nct, 0), 0))
    return pl.pallas_call(
        _merge_kernel,
        grid=(N_TOK // tm,),
        in_specs=[pl.BlockSpec((tm, D_MODEL), lambda i: (i, 0))]
        + [cspec, lspec] * N_BRANCH
        + [pl.BlockSpec((tm, N_BRANCH * D_MODEL), lambda i: (i, 0)),
           pl.BlockSpec((1, 6, D_MODEL), lambda i: (_mod_row(i, tm), 0, 0)),
           pl.BlockSpec((N_BRANCH, BRANCH_WIDTH, D_MODEL), lambda i: (0, 0, 0)),
           pl.BlockSpec((D_MODEL, D_MODEL), lambda i: (0, 0))],
        out_specs=pl.BlockSpec((tm, D_MODEL), lambda i: (i, 0)),
        out_shape=jax.ShapeDtypeStruct((N_TOK, D_MODEL), F32),
        compiler_params=pltpu.CompilerParams(dimension_semantics=("arbitrary",),
                                             vmem_limit_bytes=VMEM_LIMIT),
        name="merge",
    )(x, *[y for pair in ys for y in pair], p, mod, wb, wo)


def _mlp_kernel(x_ref, mod_ref, nw_ref, w1_ref, w2_ref, fw_ref, o_ref, h_ref, acc_ref, *, final):
    f = pl.program_id(1)

    @pl.when(f == 0)
    def _():
        _norm_mod_to(h_ref, x_ref, nw_ref, mod_ref[0, 3:4, :], mod_ref[0, 4:5, :])

    a = jnp.dot(h_ref[...], w1_ref[...], preferred_element_type=F32)
    a = jnp.square(jnp.maximum(a, 0.0)).astype(BF16)
    contrib = jnp.dot(a, w2_ref[...], preferred_element_type=F32)

    @pl.when(f == 0)
    def _():
        acc_ref[...] = contrib

    @pl.when(f != 0)
    def _():
        acc_ref[...] += contrib

    @pl.when(f == pl.num_programs(1) - 1)
    def _():
        y = x_ref[...] + mod_ref[0, 5:6, :] * acc_ref[...]
        if final:
            y = y * lax.rsqrt(jnp.mean(y * y, axis=-1, keepdims=True) + EPS) * fw_ref[...]
        o_ref[...] = y


def _mlp(x, mod, nw, w1, w2, fw, final):
    tm, tf = 1024, 512
    return pl.pallas_call(
        functools.partial(_mlp_kernel, final=final),
        grid=(N_TOK // tm, D_FF // tf),
        in_specs=[pl.BlockSpec((tm, D_MODEL), lambda i, f: (i, 0)),
                  pl.BlockSpec((1, 6, D_MODEL), lambda i, f: (_mod_row(i, tm), 0, 0)),
                  pl.BlockSpec((1, D_MODEL), lambda i, f: (0, 0)),
                  pl.BlockSpec((D_MODEL, tf), lambda i, f: (0, f)),
                  pl.BlockSpec((tf, D_MODEL), lambda i, f: (f, 0)),
                  pl.BlockSpec((1, D_MODEL), lambda i, f: (0, 0))],
        out_specs=pl.BlockSpec((tm, D_MODEL), lambda i, f: (i, 0)),
        out_shape=jax.ShapeDtypeStruct((N_TOK, D_MODEL), F32),
        scratch_shapes=[pltpu.VMEM((tm, D_MODEL), BF16), pltpu.VMEM((tm, D_MODEL), F32)],
        compiler_params=pltpu.CompilerParams(dimension_semantics=("arbitrary", "arbitrary"),
                                             vmem_limit_bytes=VMEM_LIMIT),
        name="mlp",
    )(x, mod, nw, w1, w2, fw)


HD = HEAD_DIM
N_HD = 2 * B_HEADS
CONV_PAD = 8


def _dot_nt(a, b):
    return lax.dot_general(a, b, (((1,), (1,)), ((), ())), preferred_element_type=F32)


def _dot_tn(a, b):
    return lax.dot_general(a, b, (((0,), (0,)), ((), ())), preferred_element_type=F32)


def _dot(a, b):
    return jnp.dot(a, b, preferred_element_type=F32)


def _split(x):
    hi = x.astype(BF16)
    return hi, (x - hi.astype(F32)).astype(BF16)


def _dot3(a, b):
    return _dot(a[0], b[0]) + (_dot(a[0], b[1]) + _dot(a[1], b[0]))


def _softplus(x):
    return jnp.maximum(x, 0.0) + jnp.log1p(jnp.exp(-jnp.abs(x)))


def _delta_kernel(*refs, L, has_s0):
    if has_s0:
        (q_ref, k_ref, v_ref, z_ref, ab_ref, cw_ref, prm_ref, nw_ref, s0_ref, y_ref,
         xpad, qkv_s, g_s, b_s, o_s, st_s) = refs
        sout_ref = None
    else:
        (q_ref, k_ref, v_ref, z_ref, ab_ref, cw_ref, prm_ref, nw_ref, y_ref, sout_ref,
         xpad, qkv_s, g_s, b_s, o_s, st_s) = refs
        s0_ref = None
    C = DELTA_CHUNK
    n_chunks = L // C

    zeros_pad = jnp.zeros((CONV_PAD, BRANCH_WIDTH), F32)
    xpad[0:CONV_PAD, :] = zeros_pad
    xpad[CONV_PAD + L:2 * CONV_PAD + L, :] = zeros_pad
    for idx, src in enumerate((q_ref, k_ref, v_ref)):
        xpad[CONV_PAD:CONV_PAD + L, :] = src[...]
        for r in range(n_chunks):
            acc = None
            for j in range(CONV_K):
                start = CONV_PAD + r * C + j - CONV_K // 2
                t = xpad[start:start + C, :] * cw_ref[j:j + 1, idx * BRANCH_WIDTH:(idx + 1) * BRANCH_WIDTH]
                acc = t if acc is None else acc + t
            y = acc * jax.nn.sigmoid(acc)
            if idx == 2:
                qkv_s[idx, r * C:(r + 1) * C, :] = y
            else:
                for h in range(B_HEADS):
                    yh = y[:, h * HD:(h + 1) * HD]
                    yh = yh * lax.rsqrt(jnp.sum(yh * yh, axis=-1, keepdims=True) + EPS)
                    if idx == 0:
                        yh = yh * ATTN_SCALE
                    qkv_s[idx, r * C:(r + 1) * C, h * HD:(h + 1) * HD] = yh

    ab = ab_ref[...]
    g_s[...] = -jnp.exp(prm_ref[0:1, :]) * _softplus(ab + prm_ref[1:2, :])
    b_s[...] = jax.nn.sigmoid(ab)

    if has_s0:
        st_s[...] = s0_ref[0]
    else:
        st_s[...] = jnp.zeros_like(st_s)

    ri = lax.broadcasted_iota(jnp.int32, (C, C), 0)
    ci = lax.broadcasted_iota(jnp.int32, (C, C), 1)
    eye = (ri == ci).astype(F32)
    level_masks = [(ri // 2) == (ci // 2)]
    blk = 2
    while blk < C:
        level_masks.append(((ri // (2 * blk)) == (ci // (2 * blk))) & ((ri // blk) != (ci // blk)))
        blk *= 2

    def body(n, carry):
        P = []
        for d in range(2):
            chunk = n if d == 0 else n_chunks - 1 - n
            rows = pl.ds(pl.multiple_of(chunk * C, C), C)
            incl = (ri >= ci) if d == 0 else (ri <= ci)
            strict = (ri > ci) if d == 0 else (ri < ci)
            g = g_s[rows, :]
            beta = b_s[rows, :]
            gc = jnp.dot(incl.astype(F32), g, precision=lax.Precision.HIGHEST, preferred_element_type=F32)
            gct = gc.T
            tot = gc[C - 1:C, :] if d == 0 else gc[0:1, :]
            eg = jnp.exp(gc)
            ek = jnp.exp(tot - gc)
            etot = jnp.exp(tot)
            for h in range(B_HEADS):
                c = d * B_HEADS + h
                hs = slice(h * HD, (h + 1) * HD)
                P.append(dict(d=d, c=c, hs=hs, rows=rows, incl=incl, strict=strict,
                              qh=qkv_s[0, rows, hs], kh=qkv_s[1, rows, hs], vh=qkv_s[2, rows, hs],
                              bcol=beta[:, N_HD + c:N_HD + c + 1], gcol=gc[:, c:c + 1], grow=gct[c:c + 1, :],
                              egc=eg[:, c:c + 1], ekc=ek[:, c:c + 1], etc=etot[:, c:c + 1]))
        for p in P:
            p["qk"] = _dot_nt(jnp.concatenate([p["qh"], p["kh"]], axis=0), p["kh"])
        for p in P:
            decay = jnp.exp(jnp.where(p["incl"], p["gcol"] - p["grow"], NEG))
            p["pqk"] = p["qk"][:C] * decay
            p["amat"] = jnp.where(p["strict"], p["bcol"] * p["qk"][C:] * decay, 0.0)
            p["rhs"] = jnp.concatenate([p["bcol"] * p["vh"], (p["bcol"] * p["egc"]) * p["kh"]], axis=1)
            p["tinv"] = eye - jnp.where(level_masks[0], p["amat"], 0.0)
        for lm in level_masks[1:]:
            for p in P:
                p["ts"] = _split(p["tinv"])
                p["et"] = _dot3(_split(jnp.where(lm, p["amat"], 0.0)), p["ts"])
            for p in P:
                p["tinv"] = p["tinv"] - _dot3(p["ts"], _split(p["et"]))
        for p in P:
            p["sol"] = _dot3(_split(p["tinv"]), _split(p["rhs"]))
        for p in P:
            p["s"] = st_s[p["c"]]
            p["t"] = _dot(jnp.concatenate([p["qh"] * p["egc"], p["sol"][:, HD:]], axis=0), p["s"])
        for p in P:
            p["u"] = p["sol"][:, :HD] - p["t"][C:]
            p["o"] = p["t"][:C] + _dot(p["pqk"], p["u"])
            p["s_new"] = p["etc"] * p["s"] + _dot_tn(p["kh"] * p["ekc"], p["u"])
        for p in P:
            o_s[p["d"], p["rows"], p["hs"]] = p["o"]
            st_s[p["c"]] = p["s_new"]
        return carry

    lax.fori_loop(0, n_chunks, body, 0)

    if not has_s0:
        sout_ref[0] = st_s[...]

    def out_body(r, carry):
        rows = pl.ds(pl.multiple_of(r * C, C), C)
        z = z_ref[rows, :]
        gate = z * jax.nn.sigmoid(z) * nw_ref[...]
        for h in range(B_HEADS):
            hs = slice(h * HD, (h + 1) * HD)
            o = o_s[0, rows, hs] + o_s[1, rows, hs]
            o = o * lax.rsqrt(jnp.mean(o * o, axis=-1, keepdims=True) + EPS)
            y_ref[rows, hs] = o * gate[:, hs]
        return carry

    lax.fori_loop(0, n_chunks, out_body, 0)


def _delta(p, cw, prm, nw, s0, *, L, n_seq, row_block0):
    has_s0 = s0 is not None

    def pspec(name):
        off, w = P_OFF[name]
        return pl.BlockSpec((L, w if w >= LANE else LANE), lambda i: (row_block0 + i, off // max(w, LANE)))

    in_specs = [pspec("b_q"), pspec("b_k"), pspec("b_v"), pspec("b_z"), pspec("b_ab"),
                pl.BlockSpec((CONV_K, 3 * BRANCH_WIDTH), lambda i: (0, 0)),
                pl.BlockSpec((8, LANE), lambda i: (0, 0)),
                pl.BlockSpec((1, BRANCH_WIDTH), lambda i: (0, 0))]
    args = [p, p, p, p, p, cw, prm, nw]
    st_spec = pl.BlockSpec((1, N_HD, HD, HD), lambda i: (i, 0, 0, 0))
    y_shape = jax.ShapeDtypeStruct((n_seq * L, BRANCH_WIDTH), F32)
    y_spec = pl.BlockSpec((L, BRANCH_WIDTH), lambda i: (i, 0))
    if has_s0:
        in_specs.append(st_spec)
        args.append(s0)
        out_specs, out_shape = y_spec, y_shape
    else:
        out_specs = (y_spec, st_spec)
        out_shape = (y_shape, jax.ShapeDtypeStruct((n_seq, N_HD, HD, HD), F32))
    return pl.pallas_call(
        functools.partial(_delta_kernel, L=L, has_s0=has_s0),
        grid=(n_seq,),
        in_specs=in_specs,
        out_specs=out_specs,
        out_shape=out_shape,
        scratch_shapes=[pltpu.VMEM((L + 2 * CONV_PAD, BRANCH_WIDTH), F32),
                        pltpu.VMEM((3, L, BRANCH_WIDTH), F32),
                        pltpu.VMEM((L, LANE), F32),
                        pltpu.VMEM((L, LANE), F32),
                        pltpu.VMEM((2, L, BRANCH_WIDTH), F32),
                        pltpu.VMEM((N_HD, HD, HD), F32)],
        compiler_params=pltpu.CompilerParams(dimension_semantics=("arbitrary",),
                                             vmem_limit_bytes=VMEM_LIMIT),
        name="delta_lat" if has_s0 else "delta_ctx",
    )(*args)


def _delta_params(a_log, dt_bias, norm_w):
    prm = jnp.zeros((8, LANE), F32)
    prm = prm.at[0, :N_HD].set(a_log.reshape(N_HD).astype(F32))
    prm = prm.at[1, :N_HD].set(dt_bias.reshape(N_HD).astype(F32))
    return prm, jnp.tile(norm_w.astype(F32), B_HEADS).reshape(1, BRANCH_WIDTH)


def _attend(problems):
    def scores(p):
        p["s"] = [(_dot_nt(p["q"], k) if b is None else _dot_nt(p["q"], k) + b)
                  for k, b in zip(p["ks"], p["biases"])]

    def softmax(p):
        m = None
        for s in p["s"]:
            mi = jnp.max(s, axis=-1, keepdims=True)
            m = mi if m is None else jnp.maximum(m, mi)
        if p["sink"] is not None:
            m = jnp.maximum(m, p["sink"])
        p["p"] = [jnp.exp(s - m) for s in p["s"]]
        den = None
        for pr in p["p"]:
            di = jnp.sum(pr, axis=-1, keepdims=True)
            den = di if den is None else den + di
        if p["sink"] is not None:
            den = den + jnp.exp(p["sink"] - m)
        p["den"] = den

    def values(p):
        acc = None
        for pr, v in zip(p["p"], p["vs"]):
            ai = _dot(pr.astype(v.dtype), v)
            acc = ai if acc is None else acc + ai
        return acc / p["den"]

    n = len(problems)
    outs = []
    for t in range(n + 2):
        if t < n:
            scores(problems[t])
        if 1 <= t <= n:
            softmax(problems[t - 1])
        if t >= 2:
            outs.append(values(problems[t - 2]))
    return outs


def _sink_col(sink_ref, heads, rows):
    return jnp.concatenate([jnp.full((rows, 1), sink_ref[h], F32) for h in heads], axis=0)


def _ctx_attn_kernel(sink_ref, aq_ref, ak_ref, av_ref, dq_ref, dk_ref, dv_ref, ya_ref, yd_ref):
    L = aq_ref.shape[0]
    problems = []
    for j in range(A_KV_HEADS):
        js = slice(j * HD, (j + 1) * HD)
        heads = range(j * A_GROUP, (j + 1) * A_GROUP)
        q = jnp.concatenate([aq_ref[:, h * HD:(h + 1) * HD] for h in heads], axis=0) * ATTN_SCALE
        problems.append(dict(q=q, ks=[ak_ref[:, js]], vs=[av_ref[:, js]], biases=[None],
                             sink=_sink_col(sink_ref, heads, L)))
    for h in range(D_HEADS):
        hs = slice(h * HD, (h + 1) * HD)
        problems.append(dict(q=dq_ref[:, hs] * ATTN_SCALE, ks=[dk_ref[:, hs]], vs=[dv_ref[:, hs]],
                             biases=[None], sink=None))
    outs = _attend(problems)
    for j in range(A_KV_HEADS):
        for g in range(A_GROUP):
            h = j * A_GROUP + g
            ya_ref[:, h * HD:(h + 1) * HD] = outs[j][g * L:(g + 1) * L]
    for h in range(D_HEADS):
        yd_ref[:, h * HD:(h + 1) * HD] = outs[A_KV_HEADS + h]


def _pspec(name, rows, row_block0):
    off, w = P_OFF[name]
    bw = max(w, LANE)
    return pl.BlockSpec((rows, bw), lambda i: (row_block0 + i, off // bw))


_SMEM_SPEC = pl.BlockSpec(memory_space=pltpu.SMEM)


def _ctx_attn(p, sink):
    y_shape = jax.ShapeDtypeStruct((N_CTX_TOK, BRANCH_WIDTH), F32)
    y_spec = pl.BlockSpec((SEQ, BRANCH_WIDTH), lambda i: (i, 0))
    return pl.pallas_call(
        _ctx_attn_kernel,
        grid=(BATCH,),
        in_specs=[_SMEM_SPEC] + [_pspec(n, SEQ, 0) for n in ("a_q", "a_k", "a_v", "d_q", "d_k", "d_v")],
        out_specs=(y_spec, y_spec),
        out_shape=(y_shape, y_shape),
        compiler_params=pltpu.CompilerParams(dimension_semantics=("arbitrary",),
                                             vmem_limit_bytes=VMEM_LIMIT),
        name="ctx_attn",
    )(sink, p, p, p, p, p, p)


def _rope_tables():
    t = np.arange(DEC_SEQ)
    quarter = HD // 4
    inv = ROPE_BASE ** (-np.arange(quarter, dtype=np.float64) / quarter)
    ang_r = (t // GRID_W)[:, None] * inv[None, :]
    ang_c = (t % GRID_W)[:, None] * inv[None, :]
    cos = np.concatenate([np.cos(ang_r)] * 2 + [np.cos(ang_c)] * 2, axis=1)
    sin = np.concatenate([-np.sin(ang_r), np.sin(ang_r), -np.sin(ang_c), np.sin(ang_c)], axis=1)
    return (jnp.asarray(np.tile(cos, (1, 2)), F32), jnp.asarray(np.tile(sin, (1, 2)), F32))


def _rope128(x, cos, sin):
    lane = lax.broadcasted_iota(jnp.int32, x.shape, 1)
    swapped = jnp.where((lane % 32) < 16, pltpu.roll(x, LANE - 16, 1), pltpu.roll(x, 16, 1))
    return x * cos + swapped * sin


def _win_attn_kernel(sink_ref, q_ref, k_ref, v_ref, ck_ref, cv_ref, cos_ref, sin_ref, y_ref,
                     qst_s, kr_s, vb_s, ckb_s, cvb_s):
    L = DEC_SEQ
    nb = L // A_BLOCK
    cos = cos_ref[...]
    sin = sin_ref[...]
    lane = lax.broadcasted_iota(jnp.int32, (L, LANE), 1)
    kr_s[...] = _rope128(k_ref[...], cos, sin).astype(BF16)
    vb_s[...] = v_ref[...].astype(BF16)
    ckb_s[...] = ck_ref[0, 0].astype(BF16)
    cvb_s[...] = cv_ref[0, 0].astype(BF16)
    for c in range(BRANCH_WIDTH // LANE):
        qr = _rope128(q_ref[:, c * LANE:(c + 1) * LANE], cos, sin) * ATTN_SCALE
        for e in range(2):
            h = 2 * c + e
            j = h // A_GROUP
            t = qr if e == j else pltpu.roll(qr, HD, 1)
            qst_s[h] = jnp.where((lane // HD) == j, t, 0.0).astype(BF16)

    W = 3 * A_BLOCK
    qi = lax.broadcasted_iota(jnp.int32, (A_BLOCK, W), 0)
    ki = lax.broadcasted_iota(jnp.int32, (A_BLOCK, W), 1)
    lane_b = lax.broadcasted_iota(jnp.int32, (A_BLOCK, LANE), 1)

    def body(i, carry):
        start = jnp.clip(i - 1, 0, nb - 3) * A_BLOCK
        rows = pl.ds(pl.multiple_of(i * A_BLOCK, A_BLOCK), A_BLOCK)
        krows = pl.ds(pl.multiple_of(start, A_BLOCK), W)
        ok = jnp.abs(i * A_BLOCK + qi - (start + ki)) <= A_WINDOW
        bias = jnp.where(ok, 0.0, NEG)
        problems = []
        for h in range(A_HEADS):
            problems.append(dict(q=qst_s[h, rows, :], ks=[kr_s[krows, :], ckb_s[...]],
                                 vs=[vb_s[krows, :], cvb_s[...]], biases=[bias, None],
                                 sink=jnp.full((A_BLOCK, 1), sink_ref[h], F32)))
        outs = _attend(problems)
        for c in range(BRANCH_WIDTH // LANE):
            halves = []
            for e in range(2):
                h = 2 * c + e
                halves.append(outs[h] if e == h // A_GROUP else pltpu.roll(outs[h], HD, 1))
            y_ref[rows, c * LANE:(c + 1) * LANE] = jnp.where(lane_b < HD, halves[0], halves[1])
        return carry

    lax.fori_loop(0, nb, body, 0)


def _win_attn(p, sink, cache_k, cache_v, layer, cos, sin):
    row0 = N_CTX_TOK // DEC_SEQ
    cspec = pl.BlockSpec((1, 1, PAST_LEN, LANE), lambda i: (i, layer, 0, 0))
    tspec = pl.BlockSpec((DEC_SEQ, LANE), lambda i: (0, 0))
    return pl.pallas_call(
        _win_attn_kernel,
        grid=(DEC_BATCH,),
        in_specs=[_SMEM_SPEC] + [_pspec(n, DEC_SEQ, row0) for n in ("a_q", "a_k", "a_v")]
        + [cspec, cspec, tspec, tspec],
        out_specs=pl.BlockSpec((DEC_SEQ, BRANCH_WIDTH), lambda i: (i, 0)),
        out_shape=jax.ShapeDtypeStruct((N_LAT_TOK, BRANCH_WIDTH), F32),
        scratch_shapes=[pltpu.VMEM((A_HEADS, DEC_SEQ, LANE), BF16), pltpu.VMEM((DEC_SEQ, LANE), BF16),
                        pltpu.VMEM((DEC_SEQ, LANE), BF16), pltpu.VMEM((PAST_LEN, LANE), BF16),
                        pltpu.VMEM((PAST_LEN, LANE), BF16)],
        compiler_params=pltpu.CompilerParams(dimension_semantics=("arbitrary",),
                                             vmem_limit_bytes=VMEM_LIMIT),
        name="win_attn",
    )(sink, p, p, p, cache_k, cache_v, cos, sin)


N_GRID_ROWS = DEC_SEQ // GRID_W
N_DR = 2 * NH_ROWS - 1


def _na_bias_table(rpb):
    qc = np.arange(GRID_W)[:, None]
    kc = np.arange(GRID_W)[None, :]
    wstart = np.clip(qc - NH_COLS // 2, 0, GRID_W - NH_COLS)
    ok = (kc >= wstart) & (kc < wstart + NH_COLS)
    dc = np.clip(kc - qc + NH_COLS - 1, 0, 2 * NH_COLS - 2)
    t = jnp.where(ok[None, None], rpb.astype(F32)[:, :, dc], NEG)
    return jnp.concatenate([t[:, :-1], t[:, 1:]], axis=-1)


NA_ROWS_PER_STEP = 2


def _pair_stack(q2):
    lane = lax.broadcasted_iota(jnp.int32, q2.shape, 1)
    return jnp.concatenate([jnp.where(lane < HD, q2, 0.0), jnp.where(lane >= HD, q2, 0.0)], axis=0)


def _pair_unstack(o):
    m = o.shape[0] // 2
    lane = lax.broadcasted_iota(jnp.int32, (m, LANE), 1)
    return jnp.where(lane < HD, o[:m], o[m:])


def _na_attn_kernel(q_ref, k_ref, v_ref, ck_ref, cv_ref, t_ref, y_ref, kb_s, vb_s, ckb_s, cvb_s):
    kh = NH_ROWS
    n_loc = kh * GRID_W
    n_pair = D_HEADS // 2
    kb_s[...] = k_ref[...].astype(BF16)
    vb_s[...] = v_ref[...].astype(BF16)
    ckb_s[...] = ck_ref[0, 0].astype(BF16)
    cvb_s[...] = cv_ref[0, 0].astype(BF16)

    def body(i, carry):
        problems = []
        row_sl = []
        for rr in range(NA_ROWS_PER_STEP):
            r = i * NA_ROWS_PER_STEP + rr
            rs = jnp.clip(r - kh // 2, 0, N_GRID_ROWS - kh)
            rows = pl.ds(pl.multiple_of(r * GRID_W, GRID_W), GRID_W)
            krows = pl.ds(pl.multiple_of(rs * GRID_W, GRID_W), n_loc)
            s0 = rs - r + NH_ROWS - 1
            row_sl.append(rows)
            for hp in range(n_pair):
                ps = slice(hp * LANE, (hp + 1) * LANE)
                bias = jnp.concatenate(
                    [jnp.concatenate([t_ref[2 * hp + e, s0 + 2 * w] for w in range(kh // 2)], axis=1)
                     for e in range(2)], axis=0)
                q = _pair_stack(q_ref[rows, ps] * ATTN_SCALE).astype(BF16)
                problems.append(dict(q=q, ks=[kb_s[krows, ps], ckb_s[:, ps]],
                                     vs=[vb_s[krows, ps], cvb_s[:, ps]], biases=[bias, None], sink=None))
        outs = _attend(problems)
        for rr in range(NA_ROWS_PER_STEP):
            for hp in range(n_pair):
                y_ref[row_sl[rr], hp * LANE:(hp + 1) * LANE] = _pair_unstack(outs[rr * n_pair + hp])
        return carry

    lax.fori_loop(0, N_GRID_ROWS // NA_ROWS_PER_STEP, body, 0)


def _na_attn(p, cache_k, cache_v, layer, table):
    row0 = N_CTX_TOK // DEC_SEQ
    cspec = pl.BlockSpec((1, 1, PAST_LEN, BRANCH_WIDTH), lambda i: (i, layer, 0, 0))
    return pl.pallas_call(
        _na_attn_kernel,
        grid=(DEC_BATCH,),
        in_specs=[_pspec(n, DEC_SEQ, row0) for n in ("d_q", "d_k", "d_v")]
        + [cspec, cspec, pl.BlockSpec((D_HEADS, N_DR - 1, GRID_W, LANE), lambda i: (0, 0, 0, 0))],
        out_specs=pl.BlockSpec((DEC_SEQ, BRANCH_WIDTH), lambda i: (i, 0)),
        out_shape=jax.ShapeDtypeStruct((N_LAT_TOK, BRANCH_WIDTH), F32),
        scratch_shapes=[pltpu.VMEM((DEC_SEQ, BRANCH_WIDTH), BF16), pltpu.VMEM((DEC_SEQ, BRANCH_WIDTH), BF16),
                        pltpu.VMEM((PAST_LEN, BRANCH_WIDTH), BF16), pltpu.VMEM((PAST_LEN, BRANCH_WIDTH), BF16)],
        compiler_params=pltpu.CompilerParams(dimension_semantics=("arbitrary",),
                                             vmem_limit_bytes=VMEM_LIMIT),
        name="na_attn",
    )(p, p, p, cache_k, cache_v, table)


HG = 128
N_PAIR = C_HEADS // 2
CPG = HG // HGRN_CHUNK
SUB = 8


def _hgrn_sel():
    j = np.arange(HGRN_CHUNK)[:, None, None, None]
    lane = np.arange(LANE)[None, :, None, None]
    e = np.arange(2)[None, None, :, None]
    c = np.arange(LANE)[None, None, None, :]
    sel = ((lane // HD) == e) & ((c % HGRN_CHUNK) == j)
    return jnp.asarray(sel.reshape(HGRN_CHUNK * LANE, 2 * LANE), BF16)


def _hgrn_kernel(*refs, L, has_s0):
    if has_s0:
        (q_ref, f_ref, i_ref, g_ref, lb_ref, nw_ref, sel_ref, s0_ref, y_ref,
         lf_s, ck_s, qs_s, o_s, st_s, zc_s) = refs
        sout_ref = None
    else:
        (q_ref, f_ref, i_ref, g_ref, lb_ref, nw_ref, sel_ref, y_ref, sout_ref,
         lf_s, ck_s, qs_s, o_s, st_s, zc_s) = refs
        s0_ref = None
    C = HGRN_CHUNK
    n_groups = L // HG
    R = 64

    lb = lb_ref[...]
    log_lb = jnp.log(lb)
    log_1mlb = jnp.log1p(-lb)

    def pre_body(r, carry):
        rows = pl.ds(pl.multiple_of(r * R, R), R)
        cf = f_ref[rows, :]
        b = log_1mlb - _softplus(-cf)
        lf_s[rows, :] = jnp.maximum(log_lb, b) + jnp.log1p(jnp.exp(-jnp.abs(log_lb - b)))
        ck_s[rows, :] = (1.0 - lb) * jax.nn.sigmoid(-cf)
        cq = q_ref[rows, :]
        qs_s[rows, :] = cq * jax.nn.sigmoid(cq)
        return carry

    lax.fori_loop(0, L // R, pre_body, 0)

    if has_s0:
        st_s[...] = s0_ref[0]
    else:
        st_s[...] = jnp.zeros_like(st_s)

    ri = lax.broadcasted_iota(jnp.int32, (HG, HG), 0)
    ci = lax.broadcasted_iota(jnp.int32, (HG, HG), 1)
    same_chunk = (ri // C) == (ci // C)
    same_head = (ri // HD) == (ci // HD)
    tl8 = lax.broadcasted_iota(jnp.int32, (CPG, SUB, LANE), 1)

    cum_mats = [(same_chunk & ((ci <= ri) if d == 0 else (ci >= ri))).astype(F32) for d in range(2)]

    def body(n, carry):
        P = []
        for d in range(2):
            gi = n if d == 0 else n_groups - 1 - n
            rows = pl.ds(pl.multiple_of(gi * HG, HG), HG)
            for hp in range(N_PAIR):
                fcols = slice(d * BRANCH_WIDTH + hp * LANE, d * BRANCH_WIDTH + (hp + 1) * LANE)
                hcols = slice(hp * LANE, (hp + 1) * LANE)
                P.append(dict(d=d, sidx=d * N_PAIR + hp, rows=rows, hcols=hcols,
                              lf=lf_s[rows, fcols], kk=ck_s[rows, fcols], qq=qs_s[rows, hcols],
                              vv=i_ref[rows, hcols]))
        for p in P:
            p["bcum"] = jnp.dot(cum_mats[p["d"]], p["lf"], precision=lax.Precision.HIGHEST,
                                preferred_element_type=F32)
        for p in P:
            fwd = p["d"] == 0
            b4 = p["bcum"].reshape(CPG, 2, SUB, LANE)
            k4 = p["kk"].reshape(CPG, 2, SUB, LANE)
            q4 = p["qq"].reshape(CPG, 2, SUB, LANE)
            for j in range(C):
                jh, jl = divmod(j, SUB)
                bj = jnp.broadcast_to(b4[:, jh, jl:jl + 1, :], (CPG, SUB, LANE))
                kj = jnp.broadcast_to(k4[:, jh, jl:jl + 1, :], (CPG, SUB, LANE))
                halves = []
                for th in range(2):
                    if th == jh:
                        ok = (tl8 >= jl) if fwd else (tl8 <= jl)
                        e = jnp.exp(jnp.where(ok, b4[:, th] - bj, NEG))
                    elif (th > jh) == fwd:
                        e = jnp.exp(b4[:, th] - bj)
                    else:
                        halves.append(jnp.zeros((CPG, SUB, LANE), F32))
                        continue
                    halves.append(q4[:, th] * e * kj)
                z = jnp.stack(halves, axis=1).reshape(HG, LANE)
                zc_s[p["sidx"], :, j * LANE:(j + 1) * LANE] = z.astype(BF16)
        for p in P:
            p["att"] = _dot(zc_s[p["sidx"]], sel_ref[...])
        for p in P:
            p["o_intra"] = jnp.concatenate(
                [_dot(jnp.where(same_chunk, p["att"][:, e * LANE:(e + 1) * LANE], 0.0),
                      p["vv"][:, e * HD:(e + 1) * HD]) for e in range(2)], axis=1)
            p["st"] = st_s[p["sidx"]]
            p["o_inter"] = [None] * CPG
        for cix in range(CPG):
            for p in P:
                c = cix if p["d"] == 0 else CPG - 1 - cix
                r16 = slice(c * C, (c + 1) * C)
                bc = p["bcum"][r16]
                blast = bc[C - 1:C] if p["d"] == 0 else bc[0:1]
                p["o_inter"][c] = _pair_unstack(_dot_nt(_pair_stack(p["qq"][r16] * jnp.exp(bc)), p["st"]))
                p["st"] = p["st"] * jnp.exp(blast) + _dot_tn(p["vv"][r16], p["kk"][r16] * jnp.exp(blast - bc))
        for p in P:
            o_s[p["d"], p["rows"], p["hcols"]] = p["o_intra"] + jnp.concatenate(p["o_inter"], axis=0)
            st_s[p["sidx"]] = p["st"]
        return carry

    lax.fori_loop(0, n_groups, body, 0)

    if not has_s0:
        for sidx in range(2 * N_PAIR):
            sout_ref[0, sidx] = jnp.where(same_head, st_s[sidx], 0.0)

    def out_body(r, carry):
        rows = pl.ds(pl.multiple_of(r * R, R), R)
        gate = jax.nn.sigmoid(g_ref[rows, :])
        for h in range(C_HEADS):
            hs = slice(h * HD, (h + 1) * HD)
            o = (o_s[0, rows, hs] + o_s[1, rows, hs]) * gate[:, hs]
            y_ref[rows, hs] = o * lax.rsqrt(jnp.mean(o * o, axis=-1, keepdims=True) + EPS) * nw_ref[:, hs]
        return carry

    lax.fori_loop(0, L // R, out_body, 0)


def _hgrn(p, lb, nw, sel, s0, *, L, n_seq, row_block0):
    has_s0 = s0 is not None
    in_specs = [_pspec(n, L, row_block0) for n in ("c_q", "c_f", "c_i", "c_g")] + [
        pl.BlockSpec((1, 2 * BRANCH_WIDTH), lambda i: (0, 0)),
        pl.BlockSpec((1, BRANCH_WIDTH), lambda i: (0, 0)),
        pl.BlockSpec((HGRN_CHUNK * LANE, 2 * LANE), lambda i: (0, 0))]
    args = [p, p, p, p, lb, nw, sel]
    st_spec = pl.BlockSpec((1, 2 * N_PAIR, LANE, LANE), lambda i: (i, 0, 0, 0))
    y_shape = jax.ShapeDtypeStruct((n_seq * L, BRANCH_WIDTH), F32)
    y_spec = pl.BlockSpec((L, BRANCH_WIDTH), lambda i: (i, 0))
    if has_s0:
        in_specs.append(st_spec)
        args.append(s0)
        out_specs, out_shape = y_spec, y_shape
    else:
        out_specs = (y_spec, st_spec)
        out_shape = (y_shape, jax.ShapeDtypeStruct((n_seq, 2 * N_PAIR, LANE, LANE), F32))
    return pl.pallas_call(
        functools.partial(_hgrn_kernel, L=L, has_s0=has_s0),
        grid=(n_seq,),
        in_specs=in_specs,
        out_specs=out_specs,
        out_shape=out_shape,
        scratch_shapes=[pltpu.VMEM((L, 2 * BRANCH_WIDTH), F32),
                        pltpu.VMEM((L, 2 * BRANCH_WIDTH), F32),
                        pltpu.VMEM((L, BRANCH_WIDTH), F32),
                        pltpu.VMEM((2, L, BRANCH_WIDTH), F32),
                        pltpu.VMEM((2 * N_PAIR, LANE, LANE), F32),
                        pltpu.VMEM((2 * N_PAIR, HG, HGRN_CHUNK * LANE), BF16)],
        compiler_params=pltpu.CompilerParams(dimension_semantics=("arbitrary",),
                                             vmem_limit_bytes=VMEM_LIMIT),
        name="hgrn_lat" if has_s0 else "hgrn_ctx",
    )(*args)


def _hgrn_state_in(s):
    b = s.shape[0]
    st = jnp.swapaxes(s.astype(F32), -1, -2).reshape(b, 2, N_PAIR, 2, HD, HD)
    z = jnp.zeros_like(st[:, :, :, 0])
    top = jnp.concatenate([st[:, :, :, 0], z], axis=-1)
    bot = jnp.concatenate([z, st[:, :, :, 1]], axis=-1)
    return jnp.concatenate([top, bot], axis=-2).reshape(b, 2 * N_PAIR, LANE, LANE)


def _hgrn_state_out(s):
    b = s.shape[0]
    s = s.reshape(b, 2, N_PAIR, LANE, LANE)
    blocks = jnp.stack([s[..., :HD, :HD], s[..., HD:, HD:]], axis=3)
    return jnp.swapaxes(blocks, -1, -2).reshape(b, 2, C_HEADS, HD, HD)


def _pcol(p, name):
    o, w = P_OFF[name]
    return p[..., o:o + w]


def _pack_w_in(w_in):
    parts = []
    used = 0
    for name in _P_ORDER:
        o, w = REF_OFF[name]
        parts.append(w_in[..., o:o + w])
        pw = -(-w // LANE) * LANE
        if pw != w:
            parts.append(jnp.zeros(w_in.shape[:-1] + (pw - w,), w_in.dtype))
        used += pw
    parts.append(jnp.zeros(w_in.shape[:-1] + (N_P - used,), w_in.dtype))
    return jnp.concatenate(parts, axis=-1).astype(BF16)


def kernel(x_prompt, x_sample, cache_attn_k, cache_attn_v, cache_na_k, cache_na_v, state_delta, state_hgrn,
           c, c_ctx, norm_w, ada_w, ada_b, w_in, attn_sink, delta_conv, delta_a_log, delta_dt_bias,
           delta_norm_w, hgrn_lb, hgrn_norm_w, na_rpb, w_branch, w_out, mlp_w1, mlp_w2, final_norm_w):
    lb = jnp.cumsum(jax.nn.softmax(hgrn_lb.astype(F32), axis=0), axis=0)
    lb = lb - lb[:1]

    cvec = jnp.concatenate([c_ctx[None, :], c, jnp.zeros((16 - N_MOD_ROWS, D_MODEL), F32)], axis=0)
    mod = _adaln(cvec, ada_w, ada_b).reshape(DEPTH, 16, 6, D_MODEL)

    w_in_p = _pack_w_in(w_in)
    wb = w_branch.astype(BF16)
    wo = w_out.astype(BF16)
    w1 = mlp_w1.astype(BF16)
    w2 = mlp_w2.astype(BF16)
    fw = final_norm_w.reshape(1, D_MODEL)

    cos, sin = _rope_tables()
    sel = _hgrn_sel()
    cak = cache_attn_k.reshape(DEC_BATCH, DEPTH, PAST_LEN, A_KV_HEADS * HD)
    cav = cache_attn_v.reshape(DEC_BATCH, DEPTH, PAST_LEN, A_KV_HEADS * HD)
    cnk = cache_na_k.reshape(DEC_BATCH, DEPTH, PAST_LEN, BRANCH_WIDTH)
    cnv = cache_na_v.reshape(DEC_BATCH, DEPTH, PAST_LEN, BRANCH_WIDTH)
    lat_blk0 = N_CTX_TOK // DEC_SEQ

    x = jnp.concatenate([x_prompt.reshape(N_CTX_TOK, D_MODEL), x_sample.reshape(N_LAT_TOK, D_MODEL)], axis=0)
    ak_l, av_l, nk_l, nv_l, sd_l, sh_l = [], [], [], [], [], []
    for l in range(DEPTH):
        p = _inproj(x, mod[l], norm_w[l, 0].reshape(1, D_MODEL), w_in_p[l])
        pc = p[:N_CTX_TOK]
        ak_l.append(_pcol(pc, "a_k").reshape(BATCH, SEQ, A_KV_HEADS, HD))
        av_l.append(_pcol(pc, "a_v").reshape(BATCH, SEQ, A_KV_HEADS, HD))
        nk_l.append(_pcol(pc, "d_k").reshape(BATCH, SEQ, D_HEADS, HD))
        nv_l.append(_pcol(pc, "d_v").reshape(BATCH, SEQ, D_HEADS, HD))

        ya_c, yd_c = _ctx_attn(p, attn_sink[l])
        ya_l = _win_attn(p, attn_sink[l], cak, cav, l, cos, sin)
        yd_l = _na_attn(p, cnk, cnv, l, _na_bias_table(na_rpb[l]))

        prm, dnw = _delta_params(delta_a_log[l], delta_dt_bias[l], delta_norm_w[l])
        yb_c, sd = _delta(p, delta_conv[l], prm, dnw, None, L=SEQ, n_seq=BATCH, row_block0=0)
        yb_l = _delta(p, delta_conv[l], prm, dnw, state_delta[:, l].reshape(DEC_BATCH, N_HD, HD, HD),
                      L=DEC_SEQ, n_seq=DEC_BATCH, row_block0=lat_blk0)
        sd_l.append(sd.reshape(BATCH, 2, B_HEADS, HD, HD))

        lbl = lb[l].reshape(1, 2 * BRANCH_WIDTH)
        hnw = jnp.tile(hgrn_norm_w[l].astype(F32), C_HEADS).reshape(1, BRANCH_WIDTH)
        yc_c, sh = _hgrn(p, lbl, hnw, sel, None, L=SEQ, n_seq=BATCH, row_block0=0)
        yc_l = _hgrn(p, lbl, hnw, sel, _hgrn_state_in(state_hgrn[:, l]),
                     L=DEC_SEQ, n_seq=DEC_BATCH, row_block0=lat_blk0)
        sh_l.append(_hgrn_state_out(sh))

        x = _merge(x, [(ya_c, ya_l), (yb_c, yb_l), (yc_c, yc_l), (yd_c, yd_l)], p, mod[l], wb[l], wo[l])
        x = _mlp(x, mod[l], norm_w[l, 1].reshape(1, D_MODEL), w1[l], w2[l], fw, final=(l == DEPTH - 1))

    y_prompt = x[:N_CTX_TOK].reshape(BATCH, SEQ, D_MODEL)
    y_sample = x[N_CTX_TOK:].reshape(DEC_BATCH, DEC_SEQ, D_MODEL)
    return (y_prompt, y_sample, jnp.stack(ak_l, axis=1), jnp.stack(av_l, axis=1), jnp.stack(nk_l, axis=1),
            jnp.stack(nv_l, axis=1), jnp.stack(sd_l, axis=1), jnp.stack(sh_l, axis=1))
```

```python
import functools
import math

import jax
import jax.numpy as jnp
import numpy as np
from jax import lax
from jax.experimental import pallas as pl
from jax.experimental.pallas import tpu as pltpu

F32 = jnp.float32
BF16 = jnp.bfloat16

D_MODEL = 1024
BATCH = 16
SEQ = 256
DEPTH = 4
DEC_BATCH = 8
DEC_SEQ = 1024
PAST_LEN = 512
GRID_W = 64
HEAD_DIM = 64
BRANCH_WIDTH = 512
N_BRANCH = 4
A_HEADS = 8
A_KV_HEADS = 2
A_GROUP = 4
A_WINDOW = 128
A_BLOCK = 128
Q_BLOCK = 128
B_HEADS = 8
DELTA_CHUNK = 64
CONV_K = 5
C_HEADS = 8
HGRN_CHUNK = 16
D_HEADS = 8
NH_ROWS = 8
NH_COLS = 16
NH_QCOLS = 16
NH_KCOLS = 32
D_FF = 4 * D_MODEL
ROPE_BASE = 10000.0
ATTN_SCALE = HEAD_DIM ** -0.5
EPS = 1e-6
NEG = -1e30

N_CTX_TOK = BATCH * SEQ
N_LAT_TOK = DEC_BATCH * DEC_SEQ
N_TOK = N_CTX_TOK + N_LAT_TOK
N_MOD_ROWS = 1 + DEC_BATCH

_REF_COLS = (("a_q", 512), ("a_k", 128), ("a_v", 128), ("b_q", 512), ("b_k", 512), ("b_v", 512),
             ("b_z", 512), ("b_ab", 32), ("c_q", 512), ("c_f", 1024), ("c_i", 512), ("c_g", 512),
             ("d_q", 512), ("d_k", 512), ("d_v", 512), ("g", 4096))
_P_ORDER = ("g", "a_q", "b_q", "b_k", "b_v", "b_z", "c_q", "c_f", "c_i", "c_g", "d_q", "d_k", "d_v",
            "a_k", "a_v", "b_ab")
LANE = 128
N_P = 11264


def _layout():
    ref_off, o = {}, 0
    for name, w in _REF_COLS:
        ref_off[name] = (o, w)
        o += w
    p_off, o = {}, 0
    for name in _P_ORDER:
        w = ref_off[name][1]
        p_off[name] = (o, w)
        o += -(-w // LANE) * LANE
    assert o <= N_P
    return ref_off, p_off


REF_OFF, P_OFF = _layout()

VMEM_LIMIT = 56 * 1024 * 1024


def _mod_row(i, tm):
    nct = N_CTX_TOK // tm
    tpl = DEC_SEQ // tm
    return jnp.where(i < nct, 0, 1 + (i - nct) // tpl)


def _adaln_kernel(c_ref, w_ref, b_ref, o_ref):
    c = c_ref[...]
    s = c * jax.nn.sigmoid(c)
    o_ref[0] = jnp.dot(s, w_ref[0], preferred_element_type=F32) + b_ref[0]


def _adaln(cvec, ada_w, ada_b):
    tn = 1536
    n6 = 6 * D_MODEL
    rows = cvec.shape[0]
    return pl.pallas_call(
        _adaln_kernel,
        grid=(DEPTH, n6 // tn),
        in_specs=[pl.BlockSpec((rows, D_MODEL), lambda l, j: (0, 0)),
                  pl.BlockSpec((1, D_MODEL, tn), lambda l, j: (l, 0, j)),
                  pl.BlockSpec((1, 1, tn), lambda l, j: (l, 0, j))],
        out_specs=pl.BlockSpec((1, rows, tn), lambda l, j: (l, 0, j)),
        out_shape=jax.ShapeDtypeStruct((DEPTH, rows, n6), F32),
        compiler_params=pltpu.CompilerParams(dimension_semantics=("arbitrary", "arbitrary"),
                                             vmem_limit_bytes=VMEM_LIMIT),
        name="adaln",
    )(cvec, ada_w, ada_b.reshape(DEPTH, 1, n6))


ROW_CHUNK = 128


def _norm_mod_to(h_ref, x_ref, nw_ref, shift, scale):
    n = x_ref.shape[0] // ROW_CHUNK

    def body(r, carry):
        rows = pl.ds(pl.multiple_of(r * ROW_CHUNK, ROW_CHUNK), ROW_CHUNK)
        x = x_ref[rows, :]
        y = x * lax.rsqrt(jnp.mean(x * x, axis=-1, keepdims=True) + EPS) * nw_ref[...]
        h_ref[rows, :] = (y * (1.0 + scale) + shift).astype(BF16)
        return carry

    lax.fori_loop(0, n, body, 0)


def _inproj_kernel(x_ref, mod_ref, nw_ref, w_ref, o_ref, h_ref):
    @pl.when(pl.program_id(1) == 0)
    def _():
        _norm_mod_to(h_ref, x_ref, nw_ref, mod_ref[0, 0:1, :], mod_ref[0, 1:2, :])

    o_ref[...] = jnp.dot(h_ref[...], w_ref[...], preferred_element_type=F32)


def _inproj(x, mod, nw, w):
    tm, tn = 1024, N_P // 4
    return pl.pallas_call(
        _inproj_kernel,
        grid=(N_TOK // tm, N_P // tn),
        in_specs=[pl.BlockSpec((tm, D_MODEL), lambda i, j: (i, 0)),
                  pl.BlockSpec((1, 6, D_MODEL), lambda i, j: (_mod_row(i, tm), 0, 0)),
                  pl.BlockSpec((1, D_MODEL), lambda i, j: (0, 0)),
                  pl.BlockSpec((D_MODEL, tn), lambda i, j: (0, j))],
        out_specs=pl.BlockSpec((tm, tn), lambda i, j: (i, j)),
        out_shape=jax.ShapeDtypeStruct((N_TOK, N_P), F32),
        scratch_shapes=[pltpu.VMEM((tm, D_MODEL), BF16)],
        compiler_params=pltpu.CompilerParams(dimension_semantics=("arbitrary", "arbitrary"),
                                             vmem_limit_bytes=VMEM_LIMIT),
        name="inproj",
    )(x, mod, nw, w)


MERGE_TM = 256
MERGE_CTX_TILES = N_CTX_TOK // MERGE_TM


def _merge_kernel(x_ref, *refs):
    y_refs, (g_ref, mod_ref, wb_ref, wo_ref, o_ref) = refs[:2 * N_BRANCH], refs[2 * N_BRANCH:]
    is_ctx = pl.program_id(0) < MERGE_CTX_TILES
    merged = None
    for k in range(N_BRANCH):
        y = jnp.where(is_ctx, y_refs[2 * k][...], y_refs[2 * k + 1][...])
        yp = jnp.dot(y.astype(BF16), wb_ref[k], preferred_element_type=F32)
        t = jax.nn.sigmoid(g_ref[:, k * D_MODEL:(k + 1) * D_MODEL]) * yp
        merged = t if merged is None else merged + t
    o = jnp.dot(merged.astype(BF16), wo_ref[...], preferred_element_type=F32)
    o_ref[...] = x_ref[...] + mod_ref[0, 2:3, :] * o


def _merge(x, ys, p, mod, wb, wo):
    tm = MERGE_TM
    nct = MERGE_CTX_TILES
    cspec = pl.BlockSpec((tm, BRANCH_WIDTH), lambda i: (jnp.minimum(i, nct - 1), 0))
    lspec = pl.BlockSpec((tm, BRANCH_WIDTH), lambda i: (jnp.maximum(i - nct, 0), 0))
    return pl.pallas_call(
        _merge_kernel,
        grid=(N_TOK // tm,),
        in_specs=[pl.BlockSpec((tm, D_MODEL), lambda i: (i, 0))]
        + [cspec, lspec] * N_BRANCH
        + [pl.BlockSpec((tm, N_BRANCH * D_MODEL), lambda i: (i, 0)),
           pl.BlockSpec((1, 6, D_MODEL), lambda i: (_mod_row(i, tm), 0, 0)),
           pl.BlockSpec((N_BRANCH, BRANCH_WIDTH, D_MODEL), lambda i: (0, 0, 0)),
           pl.BlockSpec((D_MODEL, D_MODEL), lambda i: (0, 0))],
        out_specs=pl.BlockSpec((tm, D_MODEL), lambda i: (i, 0)),
        out_shape=jax.ShapeDtypeStruct((N_TOK, D_MODEL), F32),
        compiler_params=pltpu.CompilerParams(dimension_semantics=("arbitrary",),
                                             vmem_limit_bytes=VMEM_LIMIT),
        name="merge",
    )(x, *[y for pair in ys for y in pair], p, mod, wb, wo)


def _mlp_kernel(x_ref, mod_ref, nw_ref, w1_ref, w2_ref, fw_ref, o_ref, h_ref, acc_ref, *, final):
    f = pl.program_id(1)

    @pl.when(f == 0)
    def _():
        _norm_mod_to(h_ref, x_ref, nw_ref, mod_ref[0, 3:4, :], mod_ref[0, 4:5, :])

    a = jnp.dot(h_ref[...], w1_ref[...], preferred_element_type=F32)
    a = jnp.square(jnp.maximum(a, 0.0)).astype(BF16)
    contrib = jnp.dot(a, w2_ref[...], preferred_element_type=F32)

    @pl.when(f == 0)
    def _():
        acc_ref[...] = contrib

    @pl.when(f != 0)
    def _():
        acc_ref[...] += contrib

    @pl.when(f == pl.num_programs(1) - 1)
    def _():
        y = x_ref[...] + mod_ref[0, 5:6, :] * acc_ref[...]
        if final:
            y = y * lax.rsqrt(jnp.mean(y * y, axis=-1, keepdims=True) + EPS) * fw_ref[...]
        o_ref[...] = y


def _mlp(x, mod, nw, w1, w2, fw, final):
    tm, tf = 1024, 1024
    return pl.pallas_call(
        functools.partial(_mlp_kernel, final=final),
        grid=(N_TOK // tm, D_FF // tf),
        in_specs=[pl.BlockSpec((tm, D_MODEL), lambda i, f: (i, 0)),
                  pl.BlockSpec((1, 6, D_MODEL), lambda i, f: (_mod_row(i, tm), 0, 0)),
                  pl.BlockSpec((1, D_MODEL), lambda i, f: (0, 0)),
                  pl.BlockSpec((D_MODEL, tf), lambda i, f: (0, f)),
                  pl.BlockSpec((tf, D_MODEL), lambda i, f: (f, 0)),
                  pl.BlockSpec((1, D_MODEL), lambda i, f: (0, 0))],
        out_specs=pl.BlockSpec((tm, D_MODEL), lambda i, f: (i, 0)),
        out_shape=jax.ShapeDtypeStruct((N_TOK, D_MODEL), F32),
        scratch_shapes=[pltpu.VMEM((tm, D_MODEL), BF16), pltpu.VMEM((tm, D_MODEL), F32)],
        compiler_params=pltpu.CompilerParams(dimension_semantics=("arbitrary", "arbitrary"),
                                             vmem_limit_bytes=VMEM_LIMIT),
        name="mlp",
    )(x, mod, nw, w1, w2, fw)


HD = HEAD_DIM
N_HD = 2 * B_HEADS
CONV_PAD = 8


def _dot_nt(a, b):
    return lax.dot_general(a, b, (((1,), (1,)), ((), ())), preferred_element_type=F32)


def _dot_tn(a, b):
    return lax.dot_general(a, b, (((0,), (0,)), ((), ())), preferred_element_type=F32)


def _dot(a, b):
    return jnp.dot(a, b, preferred_element_type=F32)


def _split(x):
    hi = x.astype(BF16)
    return hi, (x - hi.astype(F32)).astype(BF16)


def _dot3(a, b):
    return _dot(a[0], b[0]) + (_dot(a[0], b[1]) + _dot(a[1], b[0]))


def _softplus(x):
    return jnp.maximum(x, 0.0) + jnp.log1p(jnp.exp(-jnp.abs(x)))


def _delta_kernel(*refs, L, has_s0):
    if has_s0:
        (q_ref, k_ref, v_ref, z_ref, ab_ref, cw_ref, prm_ref, nw_ref, s0_ref, y_ref,
         xpad, qkv_s, g_s, b_s, o_s, st_s) = refs
        sout_ref = None
    else:
        (q_ref, k_ref, v_ref, z_ref, ab_ref, cw_ref, prm_ref, nw_ref, y_ref, sout_ref,
         xpad, qkv_s, g_s, b_s, o_s, st_s) = refs
        s0_ref = None
    C = DELTA_CHUNK
    n_chunks = L // C

    first_head = lax.broadcasted_iota(jnp.int32, (C, LANE), 1) < HD
    zeros_pad = jnp.zeros((CONV_PAD, BRANCH_WIDTH), F32)
    xpad[0:CONV_PAD, :] = zeros_pad
    xpad[CONV_PAD + L:2 * CONV_PAD + L, :] = zeros_pad
    for idx, src in enumerate((q_ref, k_ref, v_ref)):
        xpad[CONV_PAD:CONV_PAD + L, :] = src[...]
        for r in range(n_chunks):
            acc = None
            for j in range(CONV_K):
                start = CONV_PAD + r * C + j - CONV_K // 2
                t = xpad[start:start + C, :] * cw_ref[j:j + 1, idx * BRANCH_WIDTH:(idx + 1) * BRANCH_WIDTH]
                acc = t if acc is None else acc + t
            y = acc * jax.nn.sigmoid(acc)
            if idx == 2:
                qkv_s[idx, r * C:(r + 1) * C, :] = y
            else:
                for c in range(BRANCH_WIDTH // LANE):
                    y2 = y[:, c * LANE:(c + 1) * LANE]
                    sq = y2 * y2
                    s0 = jnp.sum(jnp.where(first_head, sq, 0.0), axis=-1, keepdims=True)
                    s1 = jnp.sum(jnp.where(first_head, 0.0, sq), axis=-1, keepdims=True)
                    inv = lax.rsqrt(jnp.where(first_head, s0, s1) + EPS)
                    if idx == 0:
                        inv = inv * ATTN_SCALE
                    qkv_s[idx, r * C:(r + 1) * C, c * LANE:(c + 1) * LANE] = y2 * inv

    ab = ab_ref[...]
    g_s[...] = -jnp.exp(prm_ref[0:1, :]) * _softplus(ab + prm_ref[1:2, :])
    b_s[...] = jax.nn.sigmoid(ab)

    if has_s0:
        st_s[...] = s0_ref[0]
    else:
        st_s[...] = jnp.zeros_like(st_s)

    ri = lax.broadcasted_iota(jnp.int32, (C, C), 0)
    ci = lax.broadcasted_iota(jnp.int32, (C, C), 1)
    eye = (ri == ci).astype(F32)
    level_masks = [(ri // 2) == (ci // 2)]
    blk = 2
    while blk < C:
        level_masks.append(((ri // (2 * blk)) == (ci // (2 * blk))) & ((ri // blk) != (ci // blk)))
        blk *= 2

    def body(n, carry):
        P = []
        for d in range(2):
            chunk = n if d == 0 else n_chunks - 1 - n
            rows = pl.ds(pl.multiple_of(chunk * C, C), C)
            incl = (ri >= ci) if d == 0 else (ri <= ci)
            strict = (ri > ci) if d == 0 else (ri < ci)
            g = g_s[rows, :]
            beta = b_s[rows, :]
            gc = jnp.dot(incl.astype(F32), g, precision=lax.Precision.HIGHEST, preferred_element_type=F32)
            gct = gc.T
            tot = gc[C - 1:C, :] if d == 0 else gc[0:1, :]
            eg = jnp.exp(gc)
            ek = jnp.exp(tot - gc)
            etot = jnp.exp(tot)
            for h in range(B_HEADS):
                c = d * B_HEADS + h
                hs = slice(h * HD, (h + 1) * HD)
                P.append(dict(d=d, c=c, hs=hs, rows=rows, incl=incl, strict=strict,
                              qh=qkv_s[0, rows, hs], kh=qkv_s[1, rows, hs], vh=qkv_s[2, rows, hs],
                              bcol=beta[:, N_HD + c:N_HD + c + 1], gcol=gc[:, c:c + 1], grow=gct[c:c + 1, :],
                              egc=eg[:, c:c + 1], ekc=ek[:, c:c + 1], etc=etot[:, c:c + 1]))
        for p in P:
            p["qk"] = _dot_nt(jnp.concatenate([p["qh"], p["kh"]], axis=0), p["kh"])
        for p in P:
            decay = jnp.exp(jnp.where(p["incl"], p["gcol"] - p["grow"], NEG))
            p["pqk"] = p["qk"][:C] * decay
            p["amat"] = jnp.where(p["strict"], p["bcol"] * p["qk"][C:] * decay, 0.0)
            p["rhs"] = jnp.concatenate([p["bcol"] * p["vh"], (p["bcol"] * p["egc"]) * p["kh"]], axis=1)
            p["tinv"] = eye - jnp.where(level_masks[0], p["amat"], 0.0)
        for lm in level_masks[1:]:
            for p in P:
                p["ts"] = _split(p["tinv"])
                p["et"] = _dot3(_split(jnp.where(lm, p["amat"], 0.0)), p["ts"])
            for p in P:
                p["tinv"] = p["tinv"] - _dot3(p["ts"], _split(p["et"]))
        for p in P:
            p["sol"] = _dot3(_split(p["tinv"]), _split(p["rhs"]))
        for p in P:
            p["s"] = st_s[p["c"]]
            p["t"] = _dot(jnp.concatenate([p["qh"] * p["egc"], p["sol"][:, HD:]], axis=0), p["s"])
        for p in P:
            p["u"] = p["sol"][:, :HD] - p["t"][C:]
            p["o"] = p["t"][:C] + _dot(p["pqk"], p["u"])
            p["s_new"] = p["etc"] * p["s"] + _dot_tn(p["kh"] * p["ekc"], p["u"])
        for p in P:
            o_s[p["d"], p["rows"], p["hs"]] = p["o"]
            st_s[p["c"]] = p["s_new"]
        return carry

    lax.fori_loop(0, n_chunks, body, 0)

    if not has_s0:
        sout_ref[0] = st_s[...]

    def out_body(r, carry):
        rows = pl.ds(pl.multiple_of(r * C, C), C)
        z = z_ref[rows, :]
        gate = z * jax.nn.sigmoid(z) * nw_ref[...]
        for h in range(B_HEADS):
            hs = slice(h * HD, (h + 1) * HD)
            o = o_s[0, rows, hs] + o_s[1, rows, hs]
            o = o * lax.rsqrt(jnp.mean(o * o, axis=-1, keepdims=True) + EPS)
            y_ref[rows, hs] = o * gate[:, hs]
        return carry

    lax.fori_loop(0, n_chunks, out_body, 0)


def _delta(p, cw, prm, nw, s0, *, L, n_seq, row_block0):
    has_s0 = s0 is not None

    def pspec(name):
        off, w = P_OFF[name]
        return pl.BlockSpec((L, w if w >= LANE else LANE), lambda i: (row_block0 + i, off // max(w, LANE)))

    in_specs = [pspec("b_q"), pspec("b_k"), pspec("b_v"), pspec("b_z"), pspec("b_ab"),
                pl.BlockSpec((CONV_K, 3 * BRANCH_WIDTH), lambda i: (0, 0)),
                pl.BlockSpec((8, LANE), lambda i: (0, 0)),
                pl.BlockSpec((1, BRANCH_WIDTH), lambda i: (0, 0))]
    args = [p, p, p, p, p, cw, prm, nw]
    st_spec = pl.BlockSpec((1, N_HD, HD, HD), lambda i: (i, 0, 0, 0))
    y_shape = jax.ShapeDtypeStruct((n_seq * L, BRANCH_WIDTH), F32)
    y_spec = pl.BlockSpec((L, BRANCH_WIDTH), lambda i: (i, 0))
    if has_s0:
        in_specs.append(st_spec)
        args.append(s0)
        out_specs, out_shape = y_spec, y_shape
    else:
        out_specs = (y_spec, st_spec)
        out_shape = (y_shape, jax.ShapeDtypeStruct((n_seq, N_HD, HD, HD), F32))
    return pl.pallas_call(
        functools.partial(_delta_kernel, L=L, has_s0=has_s0),
        grid=(n_seq,),
        in_specs=in_specs,
        out_specs=out_specs,
        out_shape=out_shape,
        scratch_shapes=[pltpu.VMEM((L + 2 * CONV_PAD, BRANCH_WIDTH), F32),
                        pltpu.VMEM((3, L, BRANCH_WIDTH), F32),
                        pltpu.VMEM((L, LANE), F32),
                        pltpu.VMEM((L, LANE), F32),
                        pltpu.VMEM((2, L, BRANCH_WIDTH), F32),
                        pltpu.VMEM((N_HD, HD, HD), F32)],
        compiler_params=pltpu.CompilerParams(dimension_semantics=("arbitrary",),
                                             vmem_limit_bytes=VMEM_LIMIT),
        name="delta_lat" if has_s0 else "delta_ctx",
    )(*args)


def _delta_params(a_log, dt_bias, norm_w):
    prm = jnp.zeros((8, LANE), F32)
    prm = prm.at[0, :N_HD].set(a_log.reshape(N_HD).astype(F32))
    prm = prm.at[1, :N_HD].set(dt_bias.reshape(N_HD).astype(F32))
    return prm, jnp.tile(norm_w.astype(F32), B_HEADS).reshape(1, BRANCH_WIDTH)


def _attend(problems):
    def scores(p):
        p["s"] = [(_dot_nt(p["q"], k) if b is None else _dot_nt(p["q"], k) + b)
                  for k, b in zip(p["ks"], p["biases"])]

    def softmax(p):
        m = None
        for s in p["s"]:
            mi = jnp.max(s, axis=-1, keepdims=True)
            m = mi if m is None else jnp.maximum(m, mi)
        if p["sink"] is not None:
            m = jnp.maximum(m, p["sink"])
        p["p"] = [jnp.exp(s - m) for s in p["s"]]
        den = None
        for pr in p["p"]:
            di = jnp.sum(pr, axis=-1, keepdims=True)
            den = di if den is None else den + di
        if p["sink"] is not None:
            den = den + jnp.exp(p["sink"] - m)
        p["den"] = den

    def values(p):
        acc = None
        for pr, v in zip(p["p"], p["vs"]):
            ai = _dot(pr.astype(v.dtype), v)
            acc = ai if acc is None else acc + ai
        return acc / p["den"]

    n = len(problems)
    outs = []
    for t in range(n + 2):
        if t < n:
            scores(problems[t])
        if 1 <= t <= n:
            softmax(problems[t - 1])
        if t >= 2:
            outs.append(values(problems[t - 2]))
    return outs


def _sink_col(sink_ref, heads, rows):
    return jnp.concatenate([jnp.full((rows, 1), sink_ref[h], F32) for h in heads], axis=0)


def _ctx_attn_kernel(sink_ref, aq_ref, ak_ref, av_ref, dq_ref, dk_ref, dv_ref, ya_ref, yd_ref):
    L = aq_ref.shape[0]
    lane = lax.broadcasted_iota(jnp.int32, (L, LANE), 1)
    n_tiles = BRANCH_WIDTH // LANE
    problems = []
    akb = ak_ref[...].astype(BF16)
    avb = av_ref[...].astype(BF16)
    for c in range(n_tiles):
        q2 = aq_ref[:, c * LANE:(c + 1) * LANE] * ATTN_SCALE
        for e in range(2):
            h = 2 * c + e
            j = h // A_GROUP
            t = q2 if e == j else pltpu.roll(q2, HD, 1)
            problems.append(dict(q=jnp.where((lane // HD) == j, t, 0.0).astype(BF16), ks=[akb], vs=[avb],
                                 biases=[None], sink=jnp.full((L, 1), sink_ref[h], F32)))
    for c in range(n_tiles):
        cs = slice(c * LANE, (c + 1) * LANE)
        problems.append(dict(q=_pair_stack(dq_ref[:, cs] * ATTN_SCALE).astype(BF16),
                             ks=[dk_ref[:, cs].astype(BF16)], vs=[dv_ref[:, cs].astype(BF16)],
                             biases=[None], sink=None))
    outs = _attend(problems)
    for c in range(n_tiles):
        halves = []
        for e in range(2):
            h = 2 * c + e
            halves.append(outs[h] if e == h // A_GROUP else pltpu.roll(outs[h], HD, 1))
        ya_ref[:, c * LANE:(c + 1) * LANE] = jnp.where(lane < HD, halves[0], halves[1])
        yd_ref[:, c * LANE:(c + 1) * LANE] = _pair_unstack(outs[A_HEADS + c])


def _pspec(name, rows, row_block0):
    off, w = P_OFF[name]
    bw = max(w, LANE)
    return pl.BlockSpec((rows, bw), lambda i: (row_block0 + i, off // bw))


_SMEM_SPEC = pl.BlockSpec(memory_space=pltpu.SMEM)


def _ctx_attn(p, sink):
    y_shape = jax.ShapeDtypeStruct((N_CTX_TOK, BRANCH_WIDTH), F32)
    y_spec = pl.BlockSpec((SEQ, BRANCH_WIDTH), lambda i: (i, 0))
    return pl.pallas_call(
        _ctx_attn_kernel,
        grid=(BATCH,),
        in_specs=[_SMEM_SPEC] + [_pspec(n, SEQ, 0) for n in ("a_q", "a_k", "a_v", "d_q", "d_k", "d_v")],
        out_specs=(y_spec, y_spec),
        out_shape=(y_shape, y_shape),
        compiler_params=pltpu.CompilerParams(dimension_semantics=("arbitrary",),
                                             vmem_limit_bytes=VMEM_LIMIT),
        name="ctx_attn",
    )(sink, p, p, p, p, p, p)


def _rope_tables():
    t = np.arange(DEC_SEQ)
    quarter = HD // 4
    inv = ROPE_BASE ** (-np.arange(quarter, dtype=np.float64) / quarter)
    ang_r = (t // GRID_W)[:, None] * inv[None, :]
    ang_c = (t % GRID_W)[:, None] * inv[None, :]
    cos = np.concatenate([np.cos(ang_r)] * 2 + [np.cos(ang_c)] * 2, axis=1)
    sin = np.concatenate([-np.sin(ang_r), np.sin(ang_r), -np.sin(ang_c), np.sin(ang_c)], axis=1)
    return (jnp.asarray(np.tile(cos, (1, 2)), F32), jnp.asarray(np.tile(sin, (1, 2)), F32))


def _rope128(x, cos, sin):
    lane = lax.broadcasted_iota(jnp.int32, x.shape, 1)
    swapped = jnp.where((lane % 32) < 16, pltpu.roll(x, LANE - 16, 1), pltpu.roll(x, 16, 1))
    return x * cos + swapped * sin


def _win_attn_kernel(sink_ref, q_ref, k_ref, v_ref, ck_ref, cv_ref, cos_ref, sin_ref, y_ref,
                     qst_s, kr_s, vb_s, ckb_s, cvb_s):
    L = DEC_SEQ
    nb = L // A_BLOCK
    cos = cos_ref[...]
    sin = sin_ref[...]
    lane = lax.broadcasted_iota(jnp.int32, (L, LANE), 1)
    kr_s[...] = _rope128(k_ref[...], cos, sin).astype(BF16)
    vb_s[...] = v_ref[...].astype(BF16)
    ckb_s[...] = ck_ref[0, 0].astype(BF16)
    cvb_s[...] = cv_ref[0, 0].astype(BF16)
    for c in range(BRANCH_WIDTH // LANE):
        qr = _rope128(q_ref[:, c * LANE:(c + 1) * LANE], cos, sin) * ATTN_SCALE
        for e in range(2):
            h = 2 * c + e
            j = h // A_GROUP
            t = qr if e == j else pltpu.roll(qr, HD, 1)
            qst_s[h] = jnp.where((lane // HD) == j, t, 0.0).astype(BF16)

    W = 3 * A_BLOCK
    qi = lax.broadcasted_iota(jnp.int32, (A_BLOCK, W), 0)
    ki = lax.broadcasted_iota(jnp.int32, (A_BLOCK, W), 1)
    lane_b = lax.broadcasted_iota(jnp.int32, (A_BLOCK, LANE), 1)

    def body(i, carry):
        start = jnp.clip(i - 1, 0, nb - 3) * A_BLOCK
        rows = pl.ds(pl.multiple_of(i * A_BLOCK, A_BLOCK), A_BLOCK)
        krows = pl.ds(pl.multiple_of(start, A_BLOCK), W)
        ok = jnp.abs(i * A_BLOCK + qi - (start + ki)) <= A_WINDOW
        bias = jnp.where(ok, 0.0, NEG)
        problems = []
        for h in range(A_HEADS):
            problems.append(dict(q=qst_s[h, rows, :], ks=[kr_s[krows, :], ckb_s[...]],
                                 vs=[vb_s[krows, :], cvb_s[...]], biases=[bias, None],
                                 sink=jnp.full((A_BLOCK, 1), sink_ref[h], F32)))
        outs = _attend(problems)
        for c in range(BRANCH_WIDTH // LANE):
            halves = []
            for e in range(2):
                h = 2 * c + e
                halves.append(outs[h] if e == h // A_GROUP else pltpu.roll(outs[h], HD, 1))
            y_ref[rows, c * LANE:(c + 1) * LANE] = jnp.where(lane_b < HD, halves[0], halves[1])
        return carry

    lax.fori_loop(0, nb, body, 0)


def _win_attn(p, sink, cache_k, cache_v, layer, cos, sin):
    row0 = N_CTX_TOK // DEC_SEQ
    cspec = pl.BlockSpec((1, 1, PAST_LEN, LANE), lambda i: (i, layer, 0, 0))
    tspec = pl.BlockSpec((DEC_SEQ, LANE), lambda i: (0, 0))
    return pl.pallas_call(
        _win_attn_kernel,
        grid=(DEC_BATCH,),
        in_specs=[_SMEM_SPEC] + [_pspec(n, DEC_SEQ, row0) for n in ("a_q", "a_k", "a_v")]
        + [cspec, cspec, tspec, tspec],
        out_specs=pl.BlockSpec((DEC_SEQ, BRANCH_WIDTH), lambda i: (i, 0)),
        out_shape=jax.ShapeDtypeStruct((N_LAT_TOK, BRANCH_WIDTH), F32),
        scratch_shapes=[pltpu.VMEM((A_HEADS, DEC_SEQ, LANE), BF16), pltpu.VMEM((DEC_SEQ, LANE), BF16),
                        pltpu.VMEM((DEC_SEQ, LANE), BF16), pltpu.VMEM((PAST_LEN, LANE), BF16),
                        pltpu.VMEM((PAST_LEN, LANE), BF16)],
        compiler_params=pltpu.CompilerParams(dimension_semantics=("arbitrary",),
                                             vmem_limit_bytes=VMEM_LIMIT),
        name="win_attn",
    )(sink, p, p, p, cache_k, cache_v, cos, sin)


N_GRID_ROWS = DEC_SEQ // GRID_W
N_DR = 2 * NH_ROWS - 1


def _na_bias_table(rpb):
    qc = np.arange(GRID_W)[:, None]
    kc = np.arange(GRID_W)[None, :]
    wstart = np.clip(qc - NH_COLS // 2, 0, GRID_W - NH_COLS)
    ok = (kc >= wstart) & (kc < wstart + NH_COLS)
    dc = np.clip(kc - qc + NH_COLS - 1, 0, 2 * NH_COLS - 2)
    t = jnp.where(ok[None, None], rpb.astype(F32)[:, :, dc], NEG)
    return jnp.concatenate([t[:, :-1], t[:, 1:]], axis=-1)


NA_ROWS_PER_STEP = 2


def _pair_stack(q2):
    lane = lax.broadcasted_iota(jnp.int32, q2.shape, 1)
    return jnp.concatenate([jnp.where(lane < HD, q2, 0.0), jnp.where(lane >= HD, q2, 0.0)], axis=0)


def _pair_unstack(o):
    m = o.shape[0] // 2
    lane = lax.broadcasted_iota(jnp.int32, (m, LANE), 1)
    return jnp.where(lane < HD, o[:m], o[m:])


def _na_attn_kernel(q_ref, k_ref, v_ref, ck_ref, cv_ref, t_ref, y_ref, kb_s, vb_s, ckb_s, cvb_s):
    kh = NH_ROWS
    n_loc = kh * GRID_W
    n_pair = D_HEADS // 2
    kb_s[...] = k_ref[...].astype(BF16)
    vb_s[...] = v_ref[...].astype(BF16)
    ckb_s[...] = ck_ref[0, 0].astype(BF16)
    cvb_s[...] = cv_ref[0, 0].astype(BF16)

    def body(i, carry):
        problems = []
        row_sl = []
        for rr in range(NA_ROWS_PER_STEP):
            r = i * NA_ROWS_PER_STEP + rr
            rs = jnp.clip(r - kh // 2, 0, N_GRID_ROWS - kh)
            rows = pl.ds(pl.multiple_of(r * GRID_W, GRID_W), GRID_W)
            krows = pl.ds(pl.multiple_of(rs * GRID_W, GRID_W), n_loc)
            s0 = rs - r + NH_ROWS - 1
            row_sl.append(rows)
            for hp in range(n_pair):
                ps = slice(hp * LANE, (hp + 1) * LANE)
                bias = jnp.concatenate(
                    [jnp.concatenate([t_ref[2 * hp + e, s0 + 2 * w] for w in range(kh // 2)], axis=1)
                     for e in range(2)], axis=0)
                q = _pair_stack(q_ref[rows, ps] * ATTN_SCALE).astype(BF16)
                problems.append(dict(q=q, ks=[kb_s[krows, ps], ckb_s[:, ps]],
                                     vs=[vb_s[krows, ps], cvb_s[:, ps]], biases=[bias, None], sink=None))
        outs = _attend(problems)
        for rr in range(NA_ROWS_PER_STEP):
            for hp in range(n_pair):
                y_ref[row_sl[rr], hp * LANE:(hp + 1) * LANE] = _pair_unstack(outs[rr * n_pair + hp])
        return carry

    lax.fori_loop(0, N_GRID_ROWS // NA_ROWS_PER_STEP, body, 0)


def _na_attn(p, cache_k, cache_v, layer, table):
    row0 = N_CTX_TOK // DEC_SEQ
    cspec = pl.BlockSpec((1, 1, PAST_LEN, BRANCH_WIDTH), lambda i: (i, layer, 0, 0))
    return pl.pallas_call(
        _na_attn_kernel,
        grid=(DEC_BATCH,),
        in_specs=[_pspec(n, DEC_SEQ, row0) for n in ("d_q", "d_k", "d_v")]
        + [cspec, cspec, pl.BlockSpec((D_HEADS, N_DR - 1, GRID_W, LANE), lambda i: (0, 0, 0, 0))],
        out_specs=pl.BlockSpec((DEC_SEQ, BRANCH_WIDTH), lambda i: (i, 0)),
        out_shape=jax.ShapeDtypeStruct((N_LAT_TOK, BRANCH_WIDTH), F32),
        scratch_shapes=[pltpu.VMEM((DEC_SEQ, BRANCH_WIDTH), BF16), pltpu.VMEM((DEC_SEQ, BRANCH_WIDTH), BF16),
                        pltpu.VMEM((PAST_LEN, BRANCH_WIDTH), BF16), pltpu.VMEM((PAST_LEN, BRANCH_WIDTH), BF16)],
        compiler_params=pltpu.CompilerParams(dimension_semantics=("arbitrary",),
                                             vmem_limit_bytes=VMEM_LIMIT),
        name="na_attn",
    )(p, p, p, cache_k, cache_v, table)


HG = 128
N_PAIR = C_HEADS // 2
CPG = HG // HGRN_CHUNK
SUB = 8


def _hgrn_sel():
    j = np.arange(HGRN_CHUNK)[:, None, None, None]
    lane = np.arange(LANE)[None, :, None, None]
    e = np.arange(2)[None, None, :, None]
    c = np.arange(LANE)[None, None, None, :]
    sel = ((lane // HD) == e) & ((c % HGRN_CHUNK) == j)
    return jnp.asarray(sel.reshape(HGRN_CHUNK * LANE, 2 * LANE), BF16)


def _hgrn_kernel(*refs, L, has_s0):
    if has_s0:
        (q_ref, f_ref, i_ref, g_ref, lb_ref, nw_ref, sel_ref, s0_ref, y_ref,
         lf_s, ck_s, qs_s, o_s, st_s, zc_s) = refs
        sout_ref = None
    else:
        (q_ref, f_ref, i_ref, g_ref, lb_ref, nw_ref, sel_ref, y_ref, sout_ref,
         lf_s, ck_s, qs_s, o_s, st_s, zc_s) = refs
        s0_ref = None
    C = HGRN_CHUNK
    n_groups = L // HG
    R = 64

    lb = lb_ref[...]
    log_lb = jnp.log(lb)
    log_1mlb = jnp.log1p(-lb)

    def pre_body(r, carry):
        rows = pl.ds(pl.multiple_of(r * R, R), R)
        cf = f_ref[rows, :]
        b = log_1mlb - _softplus(-cf)
        lf_s[rows, :] = jnp.maximum(log_lb, b) + jnp.log1p(jnp.exp(-jnp.abs(log_lb - b)))
        ck_s[rows, :] = (1.0 - lb) * jax.nn.sigmoid(-cf)
        cq = q_ref[rows, :]
        qs_s[rows, :] = cq * jax.nn.sigmoid(cq)
        return carry

    lax.fori_loop(0, L // R, pre_body, 0)

    if has_s0:
        st_s[...] = s0_ref[0]
    else:
        st_s[...] = jnp.zeros_like(st_s)

    ri = lax.broadcasted_iota(jnp.int32, (HG, HG), 0)
    ci = lax.broadcasted_iota(jnp.int32, (HG, HG), 1)
    same_chunk = (ri // C) == (ci // C)
    same_head = (ri // HD) == (ci // HD)
    tl8 = lax.broadcasted_iota(jnp.int32, (CPG, SUB, LANE), 1)

    cum_mats = [(same_chunk & ((ci <= ri) if d == 0 else (ci >= ri))).astype(F32) for d in range(2)]

    def body(n, carry):
        P = []
        for d in range(2):
            gi = n if d == 0 else n_groups - 1 - n
            rows = pl.ds(pl.multiple_of(gi * HG, HG), HG)
            for hp in range(N_PAIR):
                fcols = slice(d * BRANCH_WIDTH + hp * LANE, d * BRANCH_WIDTH + (hp + 1) * LANE)
                hcols = slice(hp * LANE, (hp + 1) * LANE)
                P.append(dict(d=d, sidx=d * N_PAIR + hp, rows=rows, hcols=hcols,
                              lf=lf_s[rows, fcols], kk=ck_s[rows, fcols], qq=qs_s[rows, hcols],
                              vv=i_ref[rows, hcols]))
        for p in P:
            p["bcum"] = jnp.dot(cum_mats[p["d"]], p["lf"], precision=lax.Precision.HIGHEST,
                                preferred_element_type=F32)
        for p in P:
            fwd = p["d"] == 0
            b4 = p["bcum"].reshape(CPG, 2, SUB, LANE)
            k4 = p["kk"].reshape(CPG, 2, SUB, LANE)
            q4 = p["qq"].reshape(CPG, 2, SUB, LANE)
            for j in range(C):
                jh, jl = divmod(j, SUB)
                bj = jnp.broadcast_to(b4[:, jh, jl:jl + 1, :], (CPG, SUB, LANE))
                kj = jnp.broadcast_to(k4[:, jh, jl:jl + 1, :], (CPG, SUB, LANE))
                halves = []
                for th in range(2):
                    if th == jh:
                        ok = (tl8 >= jl) if fwd else (tl8 <= jl)
                        e = jnp.exp(jnp.where(ok, b4[:, th] - bj, NEG))
                    elif (th > jh) == fwd:
                        e = jnp.exp(b4[:, th] - bj)
                    else:
                        halves.append(jnp.zeros((CPG, SUB, LANE), F32))
                        continue
                    halves.append(q4[:, th] * e * kj)
                z = jnp.stack(halves, axis=1).reshape(HG, LANE)
                zc_s[p["sidx"], :, j * LANE:(j + 1) * LANE] = z.astype(BF16)
        for p in P:
            p["att"] = _dot(zc_s[p["sidx"]], sel_ref[...])
        for p in P:
            p["o_intra"] = jnp.concatenate(
                [_dot(jnp.where(same_chunk, p["att"][:, e * LANE:(e + 1) * LANE], 0.0),
                      p["vv"][:, e * HD:(e + 1) * HD]) for e in range(2)], axis=1)
            p["st"] = st_s[p["sidx"]]
            p["o_inter"] = [None] * CPG
        for cix in range(CPG):
            for p in P:
                c = cix if p["d"] == 0 else CPG - 1 - cix
                r16 = slice(c * C, (c + 1) * C)
                bc = p["bcum"][r16]
                blast = bc[C - 1:C] if p["d"] == 0 else bc[0:1]
                p["o_inter"][c] = _pair_unstack(_dot_nt(_pair_stack(p["qq"][r16] * jnp.exp(bc)), p["st"]))
                p["st"] = p["st"] * jnp.exp(blast) + _dot_tn(p["vv"][r16], p["kk"][r16] * jnp.exp(blast - bc))
        for p in P:
            o_s[p["d"], p["rows"], p["hcols"]] = p["o_intra"] + jnp.concatenate(p["o_inter"], axis=0)
            st_s[p["sidx"]] = p["st"]
        return carry

    lax.fori_loop(0, n_groups, body, 0)

    if not has_s0:
        for sidx in range(2 * N_PAIR):
            sout_ref[0, sidx] = jnp.where(same_head, st_s[sidx], 0.0)

    def out_body(r, carry):
        rows = pl.ds(pl.multiple_of(r * R, R), R)
        gate = jax.nn.sigmoid(g_ref[rows, :])
        for h in range(C_HEADS):
            hs = slice(h * HD, (h + 1) * HD)
            o = (o_s[0, rows, hs] + o_s[1, rows, hs]) * gate[:, hs]
            y_ref[rows, hs] = o * lax.rsqrt(jnp.mean(o * o, axis=-1, keepdims=True) + EPS) * nw_ref[:, hs]
        return carry

    lax.fori_loop(0, L // R, out_body, 0)


def _hgrn(p, lb, nw, sel, s0, *, L, n_seq, row_block0):
    has_s0 = s0 is not None
    in_specs = [_pspec(n, L, row_block0) for n in ("c_q", "c_f", "c_i", "c_g")] + [
        pl.BlockSpec((1, 2 * BRANCH_WIDTH), lambda i: (0, 0)),
        pl.BlockSpec((1, BRANCH_WIDTH), lambda i: (0, 0)),
        pl.BlockSpec((HGRN_CHUNK * LANE, 2 * LANE), lambda i: (0, 0))]
    args = [p, p, p, p, lb, nw, sel]
    st_spec = pl.BlockSpec((1, 2 * N_PAIR, LANE, LANE), lambda i: (i, 0, 0, 0))
    y_shape = jax.ShapeDtypeStruct((n_seq * L, BRANCH_WIDTH), F32)
    y_spec = pl.BlockSpec((L, BRANCH_WIDTH), lambda i: (i, 0))
    if has_s0:
        in_specs.append(st_spec)
        args.append(s0)
        out_specs, out_shape = y_spec, y_shape
    else:
        out_specs = (y_spec, st_spec)
        out_shape = (y_shape, jax.ShapeDtypeStruct((n_seq, 2 * N_PAIR, LANE, LANE), F32))
    return pl.pallas_call(
        functools.partial(_hgrn_kernel, L=L, has_s0=has_s0),
        grid=(n_seq,),
        in_specs=in_specs,
        out_specs=out_specs,
        out_shape=out_shape,
        scratch_shapes=[pltpu.VMEM((L, 2 * BRANCH_WIDTH), F32),
                        pltpu.VMEM((L, 2 * BRANCH_WIDTH), F32),
                        pltpu.VMEM((L, BRANCH_WIDTH), F32),
                        pltpu.VMEM((2, L, BRANCH_WIDTH), F32),
                        pltpu.VMEM((2 * N_PAIR, LANE, LANE), F32),
                        pltpu.VMEM((2 * N_PAIR, HG, HGRN_CHUNK * LANE), BF16)],
        compiler_params=pltpu.CompilerParams(dimension_semantics=("arbitrary",),
                                             vmem_limit_bytes=VMEM_LIMIT),
        name="hgrn_lat" if has_s0 else "hgrn_ctx",
    )(*args)


def _hgrn_state_in(s):
    b = s.shape[0]
    st = jnp.swapaxes(s.astype(F32), -1, -2).reshape(b, 2, N_PAIR, 2, HD, HD)
    z = jnp.zeros_like(st[:, :, :, 0])
    top = jnp.concatenate([st[:, :, :, 0], z], axis=-1)
    bot = jnp.concatenate([z, st[:, :, :, 1]], axis=-1)
    return jnp.concatenate([top, bot], axis=-2).reshape(b, 2 * N_PAIR, LANE, LANE)


def _hgrn_state_out(s):
    b = s.shape[0]
    s = s.reshape(b, 2, N_PAIR, LANE, LANE)
    blocks = jnp.stack([s[..., :HD, :HD], s[..., HD:, HD:]], axis=3)
    return jnp.swapaxes(blocks, -1, -2).reshape(b, 2, C_HEADS, HD, HD)


def _pcol(p, name):
    o, w = P_OFF[name]
    return p[..., o:o + w]


def _pack_w_in(w_in):
    parts = []
    used = 0
    for name in _P_ORDER:
        o, w = REF_OFF[name]
        parts.append(w_in[..., o:o + w])
        pw = -(-w // LANE) * LANE
        if pw != w:
            parts.append(jnp.zeros(w_in.shape[:-1] + (pw - w,), w_in.dtype))
        used += pw
    parts.append(jnp.zeros(w_in.shape[:-1] + (N_P - used,), w_in.dtype))
    return jnp.concatenate(parts, axis=-1).astype(BF16)


def kernel(x_prompt, x_sample, cache_attn_k, cache_attn_v, cache_na_k, cache_na_v, state_delta, state_hgrn,
           c, c_ctx, norm_w, ada_w, ada_b, w_in, attn_sink, delta_conv, delta_a_log, delta_dt_bias,
           delta_norm_w, hgrn_lb, hgrn_norm_w, na_rpb, w_branch, w_out, mlp_w1, mlp_w2, final_norm_w):
    lb = jnp.cumsum(jax.nn.softmax(hgrn_lb.astype(F32), axis=0), axis=0)
    lb = lb - lb[:1]

    cvec = jnp.concatenate([c_ctx[None, :], c, jnp.zeros((16 - N_MOD_ROWS, D_MODEL), F32)], axis=0)
    mod = _adaln(cvec, ada_w, ada_b).reshape(DEPTH, 16, 6, D_MODEL)

    w_in_p = _pack_w_in(w_in)
    wb = w_branch.astype(BF16)
    wo = w_out.astype(BF16)
    w1 = mlp_w1.astype(BF16)
    w2 = mlp_w2.astype(BF16)
    fw = final_norm_w.reshape(1, D_MODEL)

    cos, sin = _rope_tables()
    sel = _hgrn_sel()
    cak = cache_attn_k.reshape(DEC_BATCH, DEPTH, PAST_LEN, A_KV_HEADS * HD)
    cav = cache_attn_v.reshape(DEC_BATCH, DEPTH, PAST_LEN, A_KV_HEADS * HD)
    cnk = cache_na_k.reshape(DEC_BATCH, DEPTH, PAST_LEN, BRANCH_WIDTH)
    cnv = cache_na_v.reshape(DEC_BATCH, DEPTH, PAST_LEN, BRANCH_WIDTH)
    lat_blk0 = N_CTX_TOK // DEC_SEQ

    x = jnp.concatenate([x_prompt.reshape(N_CTX_TOK, D_MODEL), x_sample.reshape(N_LAT_TOK, D_MODEL)], axis=0)
    ak_l, av_l, nk_l, nv_l, sd_l, sh_l = [], [], [], [], [], []
    for l in range(DEPTH):
        p = _inproj(x, mod[l], norm_w[l, 0].reshape(1, D_MODEL), w_in_p[l])
        pc = p[:N_CTX_TOK]
        ak_l.append(_pcol(pc, "a_k").reshape(BATCH, SEQ, A_KV_HEADS, HD))
        av_l.append(_pcol(pc, "a_v").reshape(BATCH, SEQ, A_KV_HEADS, HD))
        nk_l.append(_pcol(pc, "d_k").reshape(BATCH, SEQ, D_HEADS, HD))
        nv_l.append(_pcol(pc, "d_v").reshape(BATCH, SEQ, D_HEADS, HD))

        ya_c, yd_c = _ctx_attn(p, attn_sink[l])
        ya_l = _win_attn(p, attn_sink[l], cak, cav, l, cos, sin)
        yd_l = _na_attn(p, cnk, cnv, l, _na_bias_table(na_rpb[l]))

        prm, dnw = _delta_params(delta_a_log[l], delta_dt_bias[l], delta_norm_w[l])
        yb_c, sd = _delta(p, delta_conv[l], prm, dnw, None, L=SEQ, n_seq=BATCH, row_block0=0)
        yb_l = _delta(p, delta_conv[l], prm, dnw, state_delta[:, l].reshape(DEC_BATCH, N_HD, HD, HD),
                      L=DEC_SEQ, n_seq=DEC_BATCH, row_block0=lat_blk0)
        sd_l.append(sd.reshape(BATCH, 2, B_HEADS, HD, HD))

        lbl = lb[l].reshape(1, 2 * BRANCH_WIDTH)
        hnw = jnp.tile(hgrn_norm_w[l].astype(F32), C_HEADS).reshape(1, BRANCH_WIDTH)
        yc_c, sh = _hgrn(p, lbl, hnw, sel, None, L=SEQ, n_seq=BATCH, row_block0=0)
        yc_l = _hgrn(p, lbl, hnw, sel, _hgrn_state_in(state_hgrn[:, l]),
                     L=DEC_SEQ, n_seq=DEC_BATCH, row_block0=lat_blk0)
        sh_l.append(_hgrn_state_out(sh))

        x = _merge(x, [(ya_c, ya_l), (yb_c, yb_l), (yc_c, yc_l), (yd_c, yd_l)], p, mod[l], wb[l], wo[l])
        x = _mlp(x, mod[l], norm_w[l, 1].reshape(1, D_MODEL), w1[l], w2[l], fw, final=(l == DEPTH - 1))

    y_prompt = x[:N_CTX_TOK].reshape(BATCH, SEQ, D_MODEL)
    y_sample = x[N_CTX_TOK:].reshape(DEC_BATCH, DEC_SEQ, D_MODEL)
    return (y_prompt, y_sample, jnp.stack(ak_l, axis=1), jnp.stack(av_l, axis=1), jnp.stack(nk_l, axis=1),
            jnp.stack(nv_l, axis=1), jnp.stack(sd_l, axis=1), jnp.stack(sh_l, axis=1))
```

```python
import functools
import math

import jax
import jax.numpy as jnp
import numpy as np
from jax import lax
from jax.experimental import pallas as pl
from jax.experimental.pallas import tpu as pltpu

F32 = jnp.float32
BF16 = jnp.bfloat16

D_MODEL = 1024
BATCH = 16
SEQ = 256
DEPTH = 4
DEC_BATCH = 8
DEC_SEQ = 1024
PAST_LEN = 512
GRID_W = 64
HEAD_DIM = 64
BRANCH_WIDTH = 512
N_BRANCH = 4
A_HEADS = 8
A_KV_HEADS = 2
A_GROUP = 4
A_WINDOW = 128
A_BLOCK = 128
Q_BLOCK = 128
B_HEADS = 8
DELTA_CHUNK = 64
CONV_K = 5
C_HEADS = 8
HGRN_CHUNK = 16
D_HEADS = 8
NH_ROWS = 8
NH_COLS = 16
NH_QCOLS = 16
NH_KCOLS = 32
D_FF = 4 * D_MODEL
ROPE_BASE = 10000.0
ATTN_SCALE = HEAD_DIM ** -0.5
EPS = 1e-6
NEG = -1e30

N_CTX_TOK = BATCH * SEQ
N_LAT_TOK = DEC_BATCH * DEC_SEQ
N_TOK = N_CTX_TOK + N_LAT_TOK
N_MOD_ROWS = 1 + DEC_BATCH

_REF_COLS = (("a_q", 512), ("a_k", 128), ("a_v", 128), ("b_q", 512), ("b_k", 512), ("b_v", 512),
             ("b_z", 512), ("b_ab", 32), ("c_q", 512), ("c_f", 1024), ("c_i", 512), ("c_g", 512),
             ("d_q", 512), ("d_k", 512), ("d_v", 512), ("g", 4096))
_P_ORDER = ("g", "a_q", "b_q", "b_k", "b_v", "b_z", "c_q", "c_f", "c_i", "c_g", "d_q", "d_k", "d_v",
            "a_k", "a_v", "b_ab")
LANE = 128
N_P = 11264


def _layout():
    ref_off, o = {}, 0
    for name, w in _REF_COLS:
        ref_off[name] = (o, w)
        o += w
    p_off, o = {}, 0
    for name in _P_ORDER:
        w = ref_off[name][1]
        p_off[name] = (o, w)
        o += -(-w // LANE) * LANE
    assert o <= N_P
    return ref_off, p_off


REF_OFF, P_OFF = _layout()

VMEM_LIMIT = 56 * 1024 * 1024


def _mod_row(i, tm):
    nct = N_CTX_TOK // tm
    tpl = DEC_SEQ // tm
    return jnp.where(i < nct, 0, 1 + (i - nct) // tpl)


def _adaln_kernel(c_ref, w_ref, b_ref, o_ref):
    c = c_ref[...]
    s = c * jax.nn.sigmoid(c)
    o_ref[0] = jnp.dot(s, w_ref[0], preferred_element_type=F32) + b_ref[0]


def _adaln(cvec, ada_w, ada_b):
    tn = 1536
    n6 = 6 * D_MODEL
    rows = cvec.shape[0]
    return pl.pallas_call(
        _adaln_kernel,
        grid=(DEPTH, n6 // tn),
        in_specs=[pl.BlockSpec((rows, D_MODEL), lambda l, j: (0, 0)),
                  pl.BlockSpec((1, D_MODEL, tn), lambda l, j: (l, 0, j)),
                  pl.BlockSpec((1, 1, tn), lambda l, j: (l, 0, j))],
        out_specs=pl.BlockSpec((1, rows, tn), lambda l, j: (l, 0, j)),
        out_shape=jax.ShapeDtypeStruct((DEPTH, rows, n6), F32),
        compiler_params=pltpu.CompilerParams(dimension_semantics=("arbitrary", "arbitrary"),
                                             vmem_limit_bytes=VMEM_LIMIT),
        name="adaln",
    )(cvec, ada_w, ada_b.reshape(DEPTH, 1, n6))


ROW_CHUNK = 128


def _norm_mod_to(h_ref, x_ref, nw_ref, shift, scale):
    n = x_ref.shape[0] // ROW_CHUNK

    def body(r, carry):
        rows = pl.ds(pl.multiple_of(r * ROW_CHUNK, ROW_CHUNK), ROW_CHUNK)
        x = x_ref[rows, :]
        y = x * lax.rsqrt(jnp.mean(x * x, axis=-1, keepdims=True) + EPS) * nw_ref[...]
        h_ref[rows, :] = (y * (1.0 + scale) + shift).astype(BF16)
        return carry

    lax.fori_loop(0, n, body, 0)


def _inproj_kernel(x_ref, mod_ref, nw_ref, w_ref, o_ref, h_ref):
    @pl.when(pl.program_id(1) == 0)
    def _():
        _norm_mod_to(h_ref, x_ref, nw_ref, mod_ref[0, 0:1, :], mod_ref[0, 1:2, :])

    o_ref[...] = jnp.dot(h_ref[...], w_ref[0], preferred_element_type=F32)


def _inproj(x, mod, nw, w, layer):
    tm, tn = 1024, N_P // 4
    return pl.pallas_call(
        _inproj_kernel,
        grid=(N_TOK // tm, N_P // tn),
        in_specs=[pl.BlockSpec((tm, D_MODEL), lambda i, j: (i, 0)),
                  pl.BlockSpec((1, 6, D_MODEL), lambda i, j: (_mod_row(i, tm), 0, 0)),
                  pl.BlockSpec((1, D_MODEL), lambda i, j: (0, 0)),
                  pl.BlockSpec((1, D_MODEL, tn), lambda i, j: (layer, 0, j))],
        out_specs=pl.BlockSpec((tm, tn), lambda i, j: (i, j)),
        out_shape=jax.ShapeDtypeStruct((N_TOK, N_P), F32),
        scratch_shapes=[pltpu.VMEM((tm, D_MODEL), BF16)],
        compiler_params=pltpu.CompilerParams(dimension_semantics=("arbitrary", "arbitrary"),
                                             vmem_limit_bytes=VMEM_LIMIT),
        name="inproj",
    )(x, mod, nw, w)


MERGE_TM = 256
MERGE_CTX_TILES = N_CTX_TOK // MERGE_TM


def _merge_kernel(x_ref, *refs):
    y_refs, (g_ref, mod_ref, wb_ref, wo_ref, o_ref) = refs[:2 * N_BRANCH], refs[2 * N_BRANCH:]
    is_ctx = pl.program_id(0) < MERGE_CTX_TILES
    merged = None
    for k in range(N_BRANCH):
        y = jnp.where(is_ctx, y_refs[2 * k][...], y_refs[2 * k + 1][...])
        yp = jnp.dot(y.astype(BF16), wb_ref[k], preferred_element_type=F32)
        t = jax.nn.sigmoid(g_ref[:, k * D_MODEL:(k + 1) * D_MODEL]) * yp
        merged = t if merged is None else merged + t
    o = jnp.dot(merged.astype(BF16), wo_ref[...], preferred_element_type=F32)
    o_ref[...] = x_ref[...] + mod_ref[0, 2:3, :] * o


def _merge(x, ys, p, mod, wb, wo):
    tm = MERGE_TM
    nct = MERGE_CTX_TILES
    cspec = pl.BlockSpec((tm, BRANCH_WIDTH), lambda i: (jnp.minimum(i, nct - 1), 0))
    lspec = pl.BlockSpec((tm, BRANCH_WIDTH), lambda i: (jnp.maximum(i - nct, 0), 0))
    return pl.pallas_call(
        _merge_kernel,
        grid=(N_TOK // tm,),
        in_specs=[pl.BlockSpec((tm, D_MODEL), lambda i: (i, 0))]
        + [cspec, lspec] * N_BRANCH
        + [pl.BlockSpec((tm, N_BRANCH * D_MODEL), lambda i: (i, 0)),
           pl.BlockSpec((1, 6, D_MODEL), lambda i: (_mod_row(i, tm), 0, 0)),
           pl.BlockSpec((N_BRANCH, BRANCH_WIDTH, D_MODEL), lambda i: (0, 0, 0)),
           pl.BlockSpec((D_MODEL, D_MODEL), lambda i: (0, 0))],
        out_specs=pl.BlockSpec((tm, D_MODEL), lambda i: (i, 0)),
        out_shape=jax.ShapeDtypeStruct((N_TOK, D_MODEL), F32),
        compiler_params=pltpu.CompilerParams(dimension_semantics=("arbitrary",),
                                             vmem_limit_bytes=VMEM_LIMIT),
        name="merge",
    )(x, *[y for pair in ys for y in pair], p, mod, wb, wo)


def _mlp_kernel(x_ref, mod_ref, nw_ref, w1_ref, w2_ref, fw_ref, o_ref, h_ref, acc_ref, *, final):
    f = pl.program_id(1)

    @pl.when(f == 0)
    def _():
        _norm_mod_to(h_ref, x_ref, nw_ref, mod_ref[0, 3:4, :], mod_ref[0, 4:5, :])

    a = jnp.dot(h_ref[...], w1_ref[...], preferred_element_type=F32)
    a = jnp.square(jnp.maximum(a, 0.0)).astype(BF16)
    contrib = jnp.dot(a, w2_ref[...], preferred_element_type=F32)

    @pl.when(f == 0)
    def _():
        acc_ref[...] = contrib

    @pl.when(f != 0)
    def _():
        acc_ref[...] += contrib

    @pl.when(f == pl.num_programs(1) - 1)
    def _():
        y = x_ref[...] + mod_ref[0, 5:6, :] * acc_ref[...]
        if final:
            y = y * lax.rsqrt(jnp.mean(y * y, axis=-1, keepdims=True) + EPS) * fw_ref[...]
        o_ref[...] = y


def _mlp(x, mod, nw, w1, w2, fw, final):
    tm, tf = 1024, 1024
    return pl.pallas_call(
        functools.partial(_mlp_kernel, final=final),
        grid=(N_TOK // tm, D_FF // tf),
        in_specs=[pl.BlockSpec((tm, D_MODEL), lambda i, f: (i, 0)),
                  pl.BlockSpec((1, 6, D_MODEL), lambda i, f: (_mod_row(i, tm), 0, 0)),
                  pl.BlockSpec((1, D_MODEL), lambda i, f: (0, 0)),
                  pl.BlockSpec((D_MODEL, tf), lambda i, f: (0, f)),
                  pl.BlockSpec((tf, D_MODEL), lambda i, f: (f, 0)),
                  pl.BlockSpec((1, D_MODEL), lambda i, f: (0, 0))],
        out_specs=pl.BlockSpec((tm, D_MODEL), lambda i, f: (i, 0)),
        out_shape=jax.ShapeDtypeStruct((N_TOK, D_MODEL), F32),
        scratch_shapes=[pltpu.VMEM((tm, D_MODEL), BF16), pltpu.VMEM((tm, D_MODEL), F32)],
        compiler_params=pltpu.CompilerParams(dimension_semantics=("arbitrary", "arbitrary"),
                                             vmem_limit_bytes=VMEM_LIMIT),
        name="mlp",
    )(x, mod, nw, w1, w2, fw)


HD = HEAD_DIM
N_HD = 2 * B_HEADS
CONV_PAD = 8


def _dot_nt(a, b):
    return lax.dot_general(a, b, (((1,), (1,)), ((), ())), preferred_element_type=F32)


def _dot_tn(a, b):
    return lax.dot_general(a, b, (((0,), (0,)), ((), ())), preferred_element_type=F32)


def _dot(a, b):
    return jnp.dot(a, b, preferred_element_type=F32)


def _split(x):
    hi = x.astype(BF16)
    return hi, (x - hi.astype(F32)).astype(BF16)


def _dot3(a, b):
    return _dot(a[0], b[0]) + (_dot(a[0], b[1]) + _dot(a[1], b[0]))


def _softplus(x):
    return jnp.maximum(x, 0.0) + jnp.log1p(jnp.exp(-jnp.abs(x)))


def _delta_kernel(*refs, L, has_s0):
    if has_s0:
        (q_ref, k_ref, v_ref, z_ref, ab_ref, cw_ref, prm_ref, nw_ref, s0_ref, y_ref,
         xpad, qkv_s, g_s, b_s, o_s, st_s) = refs
        sout_ref = None
    else:
        (q_ref, k_ref, v_ref, z_ref, ab_ref, cw_ref, prm_ref, nw_ref, y_ref, sout_ref,
         xpad, qkv_s, g_s, b_s, o_s, st_s) = refs
        s0_ref = None
    C = DELTA_CHUNK
    n_chunks = L // C

    first_head = lax.broadcasted_iota(jnp.int32, (C, LANE), 1) < HD
    zeros_pad = jnp.zeros((CONV_PAD, BRANCH_WIDTH), F32)
    xpad[0:CONV_PAD, :] = zeros_pad
    xpad[CONV_PAD + L:2 * CONV_PAD + L, :] = zeros_pad
    for idx, src in enumerate((q_ref, k_ref, v_ref)):
        xpad[CONV_PAD:CONV_PAD + L, :] = src[...]
        for r in range(n_chunks):
            acc = None
            for j in range(CONV_K):
                start = CONV_PAD + r * C + j - CONV_K // 2
                t = xpad[start:start + C, :] * cw_ref[j:j + 1, idx * BRANCH_WIDTH:(idx + 1) * BRANCH_WIDTH]
                acc = t if acc is None else acc + t
            y = acc * jax.nn.sigmoid(acc)
            if idx == 2:
                qkv_s[idx, r * C:(r + 1) * C, :] = y
            else:
                for c in range(BRANCH_WIDTH // LANE):
                    y2 = y[:, c * LANE:(c + 1) * LANE]
                    sq = y2 * y2
                    s0 = jnp.sum(jnp.where(first_head, sq, 0.0), axis=-1, keepdims=True)
                    s1 = jnp.sum(jnp.where(first_head, 0.0, sq), axis=-1, keepdims=True)
                    inv = lax.rsqrt(jnp.where(first_head, s0, s1) + EPS)
                    if idx == 0:
                        inv = inv * ATTN_SCALE
                    qkv_s[idx, r * C:(r + 1) * C, c * LANE:(c + 1) * LANE] = y2 * inv

    ab = ab_ref[...]
    g_s[...] = -jnp.exp(prm_ref[0:1, :]) * _softplus(ab + prm_ref[1:2, :])
    b_s[...] = jax.nn.sigmoid(ab)

    if has_s0:
        st_s[...] = s0_ref[0]
    else:
        st_s[...] = jnp.zeros_like(st_s)

    ri = lax.broadcasted_iota(jnp.int32, (C, C), 0)
    ci = lax.broadcasted_iota(jnp.int32, (C, C), 1)
    eye = (ri == ci).astype(F32)
    level_masks = [(ri // 2) == (ci // 2)]
    blk = 2
    while blk < C:
        level_masks.append(((ri // (2 * blk)) == (ci // (2 * blk))) & ((ri // blk) != (ci // blk)))
        blk *= 2

    def body(n, carry):
        P = []
        for d in range(2):
            chunk = n if d == 0 else n_chunks - 1 - n
            rows = pl.ds(pl.multiple_of(chunk * C, C), C)
            incl = (ri >= ci) if d == 0 else (ri <= ci)
            strict = (ri > ci) if d == 0 else (ri < ci)
            g = g_s[rows, :]
            beta = b_s[rows, :]
            gc = jnp.dot(incl.astype(F32), g, precision=lax.Precision.HIGHEST, preferred_element_type=F32)
            gct = gc.T
            tot = gc[C - 1:C, :] if d == 0 else gc[0:1, :]
            eg = jnp.exp(gc)
            ek = jnp.exp(tot - gc)
            etot = jnp.exp(tot)
            for h in range(B_HEADS):
                c = d * B_HEADS + h
                hs = slice(h * HD, (h + 1) * HD)
                P.append(dict(d=d, c=c, hs=hs, rows=rows, incl=incl, strict=strict,
                              qh=qkv_s[0, rows, hs], kh=qkv_s[1, rows, hs], vh=qkv_s[2, rows, hs],
                              bcol=beta[:, N_HD + c:N_HD + c + 1], gcol=gc[:, c:c + 1], grow=gct[c:c + 1, :],
                              egc=eg[:, c:c + 1], ekc=ek[:, c:c + 1], etc=etot[:, c:c + 1]))
        for p in P:
            p["qk"] = _dot_nt(jnp.concatenate([p["qh"], p["kh"]], axis=0), p["kh"])
        for p in P:
            decay = jnp.exp(jnp.where(p["incl"], p["gcol"] - p["grow"], NEG))
            p["pqk"] = p["qk"][:C] * decay
            p["amat"] = jnp.where(p["strict"], p["bcol"] * p["qk"][C:] * decay, 0.0)
            p["rhs"] = jnp.concatenate([p["bcol"] * p["vh"], (p["bcol"] * p["egc"]) * p["kh"]], axis=1)
            p["tinv"] = eye - jnp.where(level_masks[0], p["amat"], 0.0)
        for lm in level_masks[1:]:
            for p in P:
                p["ts"] = _split(p["tinv"])
                p["et"] = _dot3(_split(jnp.where(lm, p["amat"], 0.0)), p["ts"])
            for p in P:
                p["tinv"] = p["tinv"] - _dot3(p["ts"], _split(p["et"]))
        for p in P:
            p["sol"] = _dot3(_split(p["tinv"]), _split(p["rhs"]))
        for p in P:
            p["s"] = st_s[p["c"]]
            p["t"] = _dot(jnp.concatenate([p["qh"] * p["egc"], p["sol"][:, HD:]], axis=0), p["s"])
        for p in P:
            p["u"] = p["sol"][:, :HD] - p["t"][C:]
            p["o"] = p["t"][:C] + _dot(p["pqk"], p["u"])
            p["s_new"] = p["etc"] * p["s"] + _dot_tn(p["kh"] * p["ekc"], p["u"])
        for p in P:
            o_s[p["d"], p["rows"], p["hs"]] = p["o"]
            st_s[p["c"]] = p["s_new"]
        return carry

    lax.fori_loop(0, n_chunks, body, 0)

    if not has_s0:
        sout_ref[0] = st_s[...]

    def out_body(r, carry):
        rows = pl.ds(pl.multiple_of(r * C, C), C)
        z = z_ref[rows, :]
        gate = z * jax.nn.sigmoid(z) * nw_ref[...]
        for h in range(B_HEADS):
            hs = slice(h * HD, (h + 1) * HD)
            o = o_s[0, rows, hs] + o_s[1, rows, hs]
            o = o * lax.rsqrt(jnp.mean(o * o, axis=-1, keepdims=True) + EPS)
            y_ref[rows, hs] = o * gate[:, hs]
        return carry

    lax.fori_loop(0, n_chunks, out_body, 0)


def _delta(p, cw, prm, nw, s0, *, L, n_seq, row_block0):
    has_s0 = s0 is not None

    in_specs = [_pspec(n, L, row_block0) for n in ("b_q", "b_k", "b_v", "b_z", "b_ab")] + [
                pl.BlockSpec((CONV_K, 3 * BRANCH_WIDTH), lambda i: (0, 0)),
                pl.BlockSpec((8, LANE), lambda i: (0, 0)),
                pl.BlockSpec((1, BRANCH_WIDTH), lambda i: (0, 0))]
    args = [p, p, p, p, p, cw, prm, nw]
    st_spec = pl.BlockSpec((1, N_HD, HD, HD), lambda i: (i, 0, 0, 0))
    y_shape = jax.ShapeDtypeStruct((n_seq * L, BRANCH_WIDTH), F32)
    y_spec = pl.BlockSpec((L, BRANCH_WIDTH), lambda i: (i, 0))
    if has_s0:
        in_specs.append(st_spec)
        args.append(s0)
        out_specs, out_shape = y_spec, y_shape
    else:
        out_specs = (y_spec, st_spec)
        out_shape = (y_shape, jax.ShapeDtypeStruct((n_seq, N_HD, HD, HD), F32))
    return pl.pallas_call(
        functools.partial(_delta_kernel, L=L, has_s0=has_s0),
        grid=(n_seq,),
        in_specs=in_specs,
        out_specs=out_specs,
        out_shape=out_shape,
        scratch_shapes=[pltpu.VMEM((L + 2 * CONV_PAD, BRANCH_WIDTH), F32),
                        pltpu.VMEM((3, L, BRANCH_WIDTH), F32),
                        pltpu.VMEM((L, LANE), F32),
                        pltpu.VMEM((L, LANE), F32),
                        pltpu.VMEM((2, L, BRANCH_WIDTH), F32),
                        pltpu.VMEM((N_HD, HD, HD), F32)],
        compiler_params=pltpu.CompilerParams(dimension_semantics=("arbitrary",),
                                             vmem_limit_bytes=VMEM_LIMIT),
        name="delta_lat" if has_s0 else "delta_ctx",
    )(*args)


def _delta_params(a_log, dt_bias, norm_w):
    prm = jnp.zeros((8, LANE), F32)
    prm = prm.at[0, :N_HD].set(a_log.reshape(N_HD).astype(F32))
    prm = prm.at[1, :N_HD].set(dt_bias.reshape(N_HD).astype(F32))
    return prm, jnp.tile(norm_w.astype(F32), B_HEADS).reshape(1, BRANCH_WIDTH)


def _attend(problems):
    def scores(p):
        p["s"] = [(_dot_nt(p["q"], k) if b is None else _dot_nt(p["q"], k) + b)
                  for k, b in zip(p["ks"], p["biases"])]

    def softmax(p):
        m = None
        for s in p["s"]:
            mi = jnp.max(s, axis=-1, keepdims=True)
            m = mi if m is None else jnp.maximum(m, mi)
        if p["sink"] is not None:
            m = jnp.maximum(m, p["sink"])
        p["p"] = [jnp.exp(s - m) for s in p["s"]]
        den = None
        for pr in p["p"]:
            di = jnp.sum(pr, axis=-1, keepdims=True)
            den = di if den is None else den + di
        if p["sink"] is not None:
            den = den + jnp.exp(p["sink"] - m)
        p["den"] = den

    def values(p):
        acc = None
        for pr, v in zip(p["p"], p["vs"]):
            ai = _dot(pr.astype(v.dtype), v)
            acc = ai if acc is None else acc + ai
        return acc / p["den"]

    n = len(problems)
    outs = []
    for t in range(n + 2):
        if t < n:
            scores(problems[t])
        if 1 <= t <= n:
            softmax(problems[t - 1])
        if t >= 2:
            outs.append(values(problems[t - 2]))
    return outs


def _ctx_attn_kernel(sink_ref, aq_ref, ak_ref, av_ref, dq_ref, dk_ref, dv_ref, ya_ref, yd_ref):
    L = aq_ref.shape[0]
    lane = lax.broadcasted_iota(jnp.int32, (L, LANE), 1)
    n_tiles = BRANCH_WIDTH // LANE
    problems = []
    akb = ak_ref[...].astype(BF16)
    avb = av_ref[...].astype(BF16)
    for c in range(n_tiles):
        q2 = aq_ref[:, c * LANE:(c + 1) * LANE] * ATTN_SCALE
        for e in range(2):
            h = 2 * c + e
            j = h // A_GROUP
            t = q2 if e == j else pltpu.roll(q2, HD, 1)
            problems.append(dict(q=jnp.where((lane // HD) == j, t, 0.0).astype(BF16), ks=[akb], vs=[avb],
                                 biases=[None], sink=jnp.full((L, 1), sink_ref[h], F32)))
    for c in range(n_tiles):
        cs = slice(c * LANE, (c + 1) * LANE)
        problems.append(dict(q=_pair_stack(dq_ref[:, cs] * ATTN_SCALE).astype(BF16),
                             ks=[dk_ref[:, cs].astype(BF16)], vs=[dv_ref[:, cs].astype(BF16)],
                             biases=[None], sink=None))
    outs = _attend(problems)
    for c in range(n_tiles):
        halves = []
        for e in range(2):
            h = 2 * c + e
            halves.append(outs[h] if e == h // A_GROUP else pltpu.roll(outs[h], HD, 1))
        ya_ref[:, c * LANE:(c + 1) * LANE] = jnp.where(lane < HD, halves[0], halves[1])
        yd_ref[:, c * LANE:(c + 1) * LANE] = _pair_unstack(outs[A_HEADS + c])


def _pspec(name, rows, row_block0):
    off, w = P_OFF[name]
    bw = max(w, LANE)
    return pl.BlockSpec((rows, bw), lambda i: (row_block0 + i, off // bw))


_SMEM_SPEC = pl.BlockSpec(memory_space=pltpu.SMEM)


def _ctx_attn(p, sink):
    y_shape = jax.ShapeDtypeStruct((N_CTX_TOK, BRANCH_WIDTH), F32)
    y_spec = pl.BlockSpec((SEQ, BRANCH_WIDTH), lambda i: (i, 0))
    return pl.pallas_call(
        _ctx_attn_kernel,
        grid=(BATCH,),
        in_specs=[_SMEM_SPEC] + [_pspec(n, SEQ, 0) for n in ("a_q", "a_k", "a_v", "d_q", "d_k", "d_v")],
        out_specs=(y_spec, y_spec),
        out_shape=(y_shape, y_shape),
        compiler_params=pltpu.CompilerParams(dimension_semantics=("arbitrary",),
                                             vmem_limit_bytes=VMEM_LIMIT),
        name="ctx_attn",
    )(sink, p, p, p, p, p, p)


def _rope_tables():
    t = np.arange(DEC_SEQ)
    quarter = HD // 4
    inv = ROPE_BASE ** (-np.arange(quarter, dtype=np.float64) / quarter)
    ang_r = (t // GRID_W)[:, None] * inv[None, :]
    ang_c = (t % GRID_W)[:, None] * inv[None, :]
    cos = np.concatenate([np.cos(ang_r)] * 2 + [np.cos(ang_c)] * 2, axis=1)
    sin = np.concatenate([-np.sin(ang_r), np.sin(ang_r), -np.sin(ang_c), np.sin(ang_c)], axis=1)
    return (jnp.asarray(np.tile(cos, (1, 2)), F32), jnp.asarray(np.tile(sin, (1, 2)), F32))


def _rope128(x, cos, sin):
    lane = lax.broadcasted_iota(jnp.int32, x.shape, 1)
    swapped = jnp.where((lane % 32) < 16, pltpu.roll(x, LANE - 16, 1), pltpu.roll(x, 16, 1))
    return x * cos + swapped * sin


def _win_attn_kernel(sink_ref, q_ref, k_ref, v_ref, ck_ref, cv_ref, cos_ref, sin_ref, y_ref,
                     qst_s, kr_s, vb_s, ckb_s, cvb_s):
    L = DEC_SEQ
    nb = L // A_BLOCK
    cos = cos_ref[...]
    sin = sin_ref[...]
    lane = lax.broadcasted_iota(jnp.int32, (L, LANE), 1)
    kr_s[...] = _rope128(k_ref[...], cos, sin).astype(BF16)
    vb_s[...] = v_ref[...].astype(BF16)
    ckb_s[...] = ck_ref[0, 0].astype(BF16)
    cvb_s[...] = cv_ref[0, 0].astype(BF16)
    for c in range(BRANCH_WIDTH // LANE):
        qr = _rope128(q_ref[:, c * LANE:(c + 1) * LANE], cos, sin) * ATTN_SCALE
        for e in range(2):
            h = 2 * c + e
            j = h // A_GROUP
            t = qr if e == j else pltpu.roll(qr, HD, 1)
            qst_s[h] = jnp.where((lane // HD) == j, t, 0.0).astype(BF16)

    W = 3 * A_BLOCK
    qi = lax.broadcasted_iota(jnp.int32, (A_BLOCK, W), 0)
    ki = lax.broadcasted_iota(jnp.int32, (A_BLOCK, W), 1)
    lane_b = lax.broadcasted_iota(jnp.int32, (A_BLOCK, LANE), 1)

    def body(i, carry):
        start = jnp.clip(i - 1, 0, nb - 3) * A_BLOCK
        rows = pl.ds(pl.multiple_of(i * A_BLOCK, A_BLOCK), A_BLOCK)
        krows = pl.ds(pl.multiple_of(start, A_BLOCK), W)
        ok = jnp.abs(i * A_BLOCK + qi - (start + ki)) <= A_WINDOW
        bias = jnp.where(ok, 0.0, NEG)
        problems = []
        for h in range(A_HEADS):
            problems.append(dict(q=qst_s[h, rows, :], ks=[kr_s[krows, :], ckb_s[...]],
                                 vs=[vb_s[krows, :], cvb_s[...]], biases=[bias, None],
                                 sink=jnp.full((A_BLOCK, 1), sink_ref[h], F32)))
        outs = _attend(problems)
        for c in range(BRANCH_WIDTH // LANE):
            halves = []
            for e in range(2):
                h = 2 * c + e
                halves.append(outs[h] if e == h // A_GROUP else pltpu.roll(outs[h], HD, 1))
            y_ref[rows, c * LANE:(c + 1) * LANE] = jnp.where(lane_b < HD, halves[0], halves[1])
        return carry

    lax.fori_loop(0, nb, body, 0)


def _win_attn(p, sink, cache_k, cache_v, layer, cos, sin):
    row0 = N_CTX_TOK // DEC_SEQ
    cspec = pl.BlockSpec((1, 1, PAST_LEN, LANE), lambda i: (i, layer, 0, 0))
    tspec = pl.BlockSpec((DEC_SEQ, LANE), lambda i: (0, 0))
    return pl.pallas_call(
        _win_attn_kernel,
        grid=(DEC_BATCH,),
        in_specs=[_SMEM_SPEC] + [_pspec(n, DEC_SEQ, row0) for n in ("a_q", "a_k", "a_v")]
        + [cspec, cspec, tspec, tspec],
        out_specs=pl.BlockSpec((DEC_SEQ, BRANCH_WIDTH), lambda i: (i, 0)),
        out_shape=jax.ShapeDtypeStruct((N_LAT_TOK, BRANCH_WIDTH), F32),
        scratch_shapes=[pltpu.VMEM((A_HEADS, DEC_SEQ, LANE), BF16), pltpu.VMEM((DEC_SEQ, LANE), BF16),
                        pltpu.VMEM((DEC_SEQ, LANE), BF16), pltpu.VMEM((PAST_LEN, LANE), BF16),
                        pltpu.VMEM((PAST_LEN, LANE), BF16)],
        compiler_params=pltpu.CompilerParams(dimension_semantics=("arbitrary",),
                                             vmem_limit_bytes=VMEM_LIMIT),
        name="win_attn",
    )(sink, p, p, p, cache_k, cache_v, cos, sin)


N_GRID_ROWS = DEC_SEQ // GRID_W
N_DR = 2 * NH_ROWS - 1


def _na_bias_table(rpb):
    qc = np.arange(GRID_W)[:, None]
    kc = np.arange(GRID_W)[None, :]
    wstart = np.clip(qc - NH_COLS // 2, 0, GRID_W - NH_COLS)
    ok = (kc >= wstart) & (kc < wstart + NH_COLS)
    dc = np.clip(kc - qc + NH_COLS - 1, 0, 2 * NH_COLS - 2)
    t = jnp.where(ok[None, None], rpb.astype(F32)[:, :, dc], NEG)
    return jnp.concatenate([t[:, :-1], t[:, 1:]], axis=-1)


NA_ROWS_PER_STEP = 2


def _pair_stack(q2):
    lane = lax.broadcasted_iota(jnp.int32, q2.shape, 1)
    return jnp.concatenate([jnp.where(lane < HD, q2, 0.0), jnp.where(lane >= HD, q2, 0.0)], axis=0)


def _pair_unstack(o):
    m = o.shape[0] // 2
    lane = lax.broadcasted_iota(jnp.int32, (m, LANE), 1)
    return jnp.where(lane < HD, o[:m], o[m:])


def _na_attn_kernel(q_ref, k_ref, v_ref, ck_ref, cv_ref, t_ref, y_ref, kb_s, vb_s, ckb_s, cvb_s):
    kh = NH_ROWS
    n_loc = kh * GRID_W
    n_pair = D_HEADS // 2
    kb_s[...] = k_ref[...].astype(BF16)
    vb_s[...] = v_ref[...].astype(BF16)
    ckb_s[...] = ck_ref[0, 0].astype(BF16)
    cvb_s[...] = cv_ref[0, 0].astype(BF16)

    def body(i, carry):
        problems = []
        row_sl = []
        for rr in range(NA_ROWS_PER_STEP):
            r = i * NA_ROWS_PER_STEP + rr
            rs = jnp.clip(r - kh // 2, 0, N_GRID_ROWS - kh)
            rows = pl.ds(pl.multiple_of(r * GRID_W, GRID_W), GRID_W)
            krows = pl.ds(pl.multiple_of(rs * GRID_W, GRID_W), n_loc)
            s0 = rs - r + NH_ROWS - 1
            row_sl.append(rows)
            for hp in range(n_pair):
                ps = slice(hp * LANE, (hp + 1) * LANE)
                bias = jnp.concatenate(
                    [jnp.concatenate([t_ref[2 * hp + e, s0 + 2 * w] for w in range(kh // 2)], axis=1)
                     for e in range(2)], axis=0)
                q = _pair_stack(q_ref[rows, ps] * ATTN_SCALE).astype(BF16)
                problems.append(dict(q=q, ks=[kb_s[krows, ps], ckb_s[:, ps]],
                                     vs=[vb_s[krows, ps], cvb_s[:, ps]], biases=[bias, None], sink=None))
        outs = _attend(problems)
        for rr in range(NA_ROWS_PER_STEP):
            for hp in range(n_pair):
                y_ref[row_sl[rr], hp * LANE:(hp + 1) * LANE] = _pair_unstack(outs[rr * n_pair + hp])
        return carry

    lax.fori_loop(0, N_GRID_ROWS // NA_ROWS_PER_STEP, body, 0)


def _na_attn(p, cache_k, cache_v, layer, table):
    row0 = N_CTX_TOK // DEC_SEQ
    cspec = pl.BlockSpec((1, 1, PAST_LEN, BRANCH_WIDTH), lambda i: (i, layer, 0, 0))
    return pl.pallas_call(
        _na_attn_kernel,
        grid=(DEC_BATCH,),
        in_specs=[_pspec(n, DEC_SEQ, row0) for n in ("d_q", "d_k", "d_v")]
        + [cspec, cspec, pl.BlockSpec((D_HEADS, N_DR - 1, GRID_W, LANE), lambda i: (0, 0, 0, 0))],
        out_specs=pl.BlockSpec((DEC_SEQ, BRANCH_WIDTH), lambda i: (i, 0)),
        out_shape=jax.ShapeDtypeStruct((N_LAT_TOK, BRANCH_WIDTH), F32),
        scratch_shapes=[pltpu.VMEM((DEC_SEQ, BRANCH_WIDTH), BF16), pltpu.VMEM((DEC_SEQ, BRANCH_WIDTH), BF16),
                        pltpu.VMEM((PAST_LEN, BRANCH_WIDTH), BF16), pltpu.VMEM((PAST_LEN, BRANCH_WIDTH), BF16)],
        compiler_params=pltpu.CompilerParams(dimension_semantics=("arbitrary",),
                                             vmem_limit_bytes=VMEM_LIMIT),
        name="na_attn",
    )(p, p, p, cache_k, cache_v, table)


HG = 128
N_PAIR = C_HEADS // 2
CPG = HG // HGRN_CHUNK
SUB = 8
LOG2_E = 1.4426950408889634


def _hgrn_sel():
    j = np.arange(HGRN_CHUNK)[:, None, None, None]
    lane = np.arange(LANE)[None, :, None, None]
    e = np.arange(2)[None, None, :, None]
    c = np.arange(LANE)[None, None, None, :]
    sel = ((lane // HD) == e) & ((c % HGRN_CHUNK) == j)
    return jnp.asarray(sel.reshape(HGRN_CHUNK * LANE, 2 * LANE), BF16)


def _hgrn_kernel(*refs, L, has_s0):
    if has_s0:
        (q_ref, f_ref, i_ref, g_ref, lb_ref, nw_ref, sel_ref, s0_ref, y_ref,
         lf_s, ck_s, qs_s, o_s, st_s, zc_s) = refs
        sout_ref = None
    else:
        (q_ref, f_ref, i_ref, g_ref, lb_ref, nw_ref, sel_ref, y_ref, sout_ref,
         lf_s, ck_s, qs_s, o_s, st_s, zc_s) = refs
        s0_ref = None
    C = HGRN_CHUNK
    n_groups = L // HG
    R = 64

    lb = lb_ref[...]
    log_lb = jnp.log(lb)
    log_1mlb = jnp.log1p(-lb)

    def pre_body(r, carry):
        rows = pl.ds(pl.multiple_of(r * R, R), R)
        cf = f_ref[rows, :]
        b = log_1mlb - _softplus(-cf)
        lf_s[rows, :] = jnp.maximum(log_lb, b) + jnp.log1p(jnp.exp(-jnp.abs(log_lb - b)))
        ck_s[rows, :] = (1.0 - lb) * jax.nn.sigmoid(-cf)
        cq = q_ref[rows, :]
        qs_s[rows, :] = cq * jax.nn.sigmoid(cq)
        return carry

    lax.fori_loop(0, L // R, pre_body, 0)

    if has_s0:
        st_s[...] = s0_ref[0]
    else:
        st_s[...] = jnp.zeros_like(st_s)

    ri = lax.broadcasted_iota(jnp.int32, (HG, HG), 0)
    ci = lax.broadcasted_iota(jnp.int32, (HG, HG), 1)
    same_chunk = (ri // C) == (ci // C)
    same_head = (ri // HD) == (ci // HD)
    tl8 = lax.broadcasted_iota(jnp.int32, (CPG, SUB, LANE), 1)

    cum_mats = [(same_chunk & ((ci <= ri) if d == 0 else (ci >= ri))).astype(F32) for d in range(2)]

    def body(n, carry):
        P = []
        for d in range(2):
            gi = n if d == 0 else n_groups - 1 - n
            rows = pl.ds(pl.multiple_of(gi * HG, HG), HG)
            for hp in range(N_PAIR):
                fcols = slice(d * BRANCH_WIDTH + hp * LANE, d * BRANCH_WIDTH + (hp + 1) * LANE)
                hcols = slice(hp * LANE, (hp + 1) * LANE)
                P.append(dict(d=d, sidx=d * N_PAIR + hp, rows=rows, hcols=hcols,
                              lf=lf_s[rows, fcols], kk=ck_s[rows, fcols], qq=qs_s[rows, hcols],
                              vv=i_ref[rows, hcols]))
        for p in P:
            p["bcum"] = jnp.dot(cum_mats[p["d"]], p["lf"], precision=lax.Precision.HIGHEST,
                                preferred_element_type=F32)
        for p in P:
            fwd = p["d"] == 0
            b4 = (p["bcum"] * LOG2_E).reshape(CPG, 2, SUB, LANE)
            k4 = p["kk"].reshape(CPG, 2, SUB, LANE)
            q4 = p["qq"].reshape(CPG, 2, SUB, LANE)
            for j in range(C):
                jh, jl = divmod(j, SUB)
                bj = jnp.broadcast_to(b4[:, jh, jl:jl + 1, :], (CPG, SUB, LANE))
                kj = jnp.broadcast_to(k4[:, jh, jl:jl + 1, :], (CPG, SUB, LANE))
                halves = []
                for th in range(2):
                    if th == jh:
                        ok = (tl8 >= jl) if fwd else (tl8 <= jl)
                        e = jnp.exp2(jnp.where(ok, b4[:, th] - bj, NEG))
                    elif (th > jh) == fwd:
                        e = jnp.exp2(b4[:, th] - bj)
                    else:
                        halves.append(jnp.zeros((CPG, SUB, LANE), F32))
                        continue
                    halves.append(q4[:, th] * e * kj)
                z = jnp.stack(halves, axis=1).reshape(HG, LANE)
                zc_s[p["sidx"], :, j * LANE:(j + 1) * LANE] = z.astype(BF16)
        for p in P:
            p["att"] = _dot(zc_s[p["sidx"]], sel_ref[...])
        for p in P:
            p["o_intra"] = jnp.concatenate(
                [_dot(jnp.where(same_chunk, p["att"][:, e * LANE:(e + 1) * LANE], 0.0),
                      p["vv"][:, e * HD:(e + 1) * HD]) for e in range(2)], axis=1)
            p["st"] = st_s[p["sidx"]]
            p["o_inter"] = [None] * CPG
        for cix in range(CPG):
            for p in P:
                c = cix if p["d"] == 0 else CPG - 1 - cix
                r16 = slice(c * C, (c + 1) * C)
                bc = p["bcum"][r16]
                blast = bc[C - 1:C] if p["d"] == 0 else bc[0:1]
                p["o_inter"][c] = _pair_unstack(_dot_nt(_pair_stack(p["qq"][r16] * jnp.exp(bc)), p["st"]))
                p["st"] = p["st"] * jnp.exp(blast) + _dot_tn(p["vv"][r16], p["kk"][r16] * jnp.exp(blast - bc))
        for p in P:
            o_s[p["d"], p["rows"], p["hcols"]] = p["o_intra"] + jnp.concatenate(p["o_inter"], axis=0)
            st_s[p["sidx"]] = p["st"]
        return carry

    lax.fori_loop(0, n_groups, body, 0)

    if not has_s0:
        for sidx in range(2 * N_PAIR):
            sout_ref[0, sidx] = jnp.where(same_head, st_s[sidx], 0.0)

    def out_body(r, carry):
        rows = pl.ds(pl.multiple_of(r * R, R), R)
        gate = jax.nn.sigmoid(g_ref[rows, :])
        for h in range(C_HEADS):
            hs = slice(h * HD, (h + 1) * HD)
            o = (o_s[0, rows, hs] + o_s[1, rows, hs]) * gate[:, hs]
            y_ref[rows, hs] = o * lax.rsqrt(jnp.mean(o * o, axis=-1, keepdims=True) + EPS) * nw_ref[:, hs]
        return carry

    lax.fori_loop(0, L // R, out_body, 0)


def _hgrn(p, lb, nw, sel, s0, *, L, n_seq, row_block0):
    has_s0 = s0 is not None
    in_specs = [_pspec(n, L, row_block0) for n in ("c_q", "c_f", "c_i", "c_g")] + [
        pl.BlockSpec((1, 2 * BRANCH_WIDTH), lambda i: (0, 0)),
        pl.BlockSpec((1, BRANCH_WIDTH), lambda i: (0, 0)),
        pl.BlockSpec((HGRN_CHUNK * LANE, 2 * LANE), lambda i: (0, 0))]
    args = [p, p, p, p, lb, nw, sel]
    st_spec = pl.BlockSpec((1, 2 * N_PAIR, LANE, LANE), lambda i: (i, 0, 0, 0))
    y_shape = jax.ShapeDtypeStruct((n_seq * L, BRANCH_WIDTH), F32)
    y_spec = pl.BlockSpec((L, BRANCH_WIDTH), lambda i: (i, 0))
    if has_s0:
        in_specs.append(st_spec)
        args.append(s0)
        out_specs, out_shape = y_spec, y_shape
    else:
        out_specs = (y_spec, st_spec)
        out_shape = (y_shape, jax.ShapeDtypeStruct((n_seq, 2 * N_PAIR, LANE, LANE), F32))
    return pl.pallas_call(
        functools.partial(_hgrn_kernel, L=L, has_s0=has_s0),
        grid=(n_seq,),
        in_specs=in_specs,
        out_specs=out_specs,
        out_shape=out_shape,
        scratch_shapes=[pltpu.VMEM((L, 2 * BRANCH_WIDTH), F32),
                        pltpu.VMEM((L, 2 * BRANCH_WIDTH), F32),
                        pltpu.VMEM((L, BRANCH_WIDTH), F32),
                        pltpu.VMEM((2, L, BRANCH_WIDTH), F32),
                        pltpu.VMEM((2 * N_PAIR, LANE, LANE), F32),
                        pltpu.VMEM((2 * N_PAIR, HG, HGRN_CHUNK * LANE), BF16)],
        compiler_params=pltpu.CompilerParams(dimension_semantics=("arbitrary",),
                                             vmem_limit_bytes=VMEM_LIMIT),
        name="hgrn_lat" if has_s0 else "hgrn_ctx",
    )(*args)


def _hgrn_state_in(s):
    b = s.shape[0]
    st = jnp.swapaxes(s.astype(F32), -1, -2).reshape(b, 2, N_PAIR, 2, HD, HD)
    z = jnp.zeros_like(st[:, :, :, 0])
    top = jnp.concatenate([st[:, :, :, 0], z], axis=-1)
    bot = jnp.concatenate([z, st[:, :, :, 1]], axis=-1)
    return jnp.concatenate([top, bot], axis=-2).reshape(b, 2 * N_PAIR, LANE, LANE)


def _hgrn_state_out(s):
    b = s.shape[0]
    s = s.reshape(b, 2, N_PAIR, LANE, LANE)
    blocks = jnp.stack([s[..., :HD, :HD], s[..., HD:, HD:]], axis=3)
    return jnp.swapaxes(blocks, -1, -2).reshape(b, 2, C_HEADS, HD, HD)


def _pcol(p, name):
    o, w = P_OFF[name]
    return p[..., o:o + w]


def _pack_w_in(w_in):
    parts = []
    used = 0
    for name in _P_ORDER:
        o, w = REF_OFF[name]
        parts.append(w_in[..., o:o + w])
        pw = -(-w // LANE) * LANE
        if pw != w:
            parts.append(jnp.zeros(w_in.shape[:-1] + (pw - w,), w_in.dtype))
        used += pw
    parts.append(jnp.zeros(w_in.shape[:-1] + (N_P - used,), w_in.dtype))
    return jnp.concatenate(parts, axis=-1).astype(BF16)


def kernel(x_prompt, x_sample, cache_attn_k, cache_attn_v, cache_na_k, cache_na_v, state_delta, state_hgrn,
           c, c_ctx, norm_w, ada_w, ada_b, w_in, attn_sink, delta_conv, delta_a_log, delta_dt_bias,
           delta_norm_w, hgrn_lb, hgrn_norm_w, na_rpb, w_branch, w_out, mlp_w1, mlp_w2, final_norm_w):
    lb = jnp.cumsum(jax.nn.softmax(hgrn_lb.astype(F32), axis=0), axis=0)
    lb = lb - lb[:1]

    cvec = jnp.concatenate([c_ctx[None, :], c, jnp.zeros((16 - N_MOD_ROWS, D_MODEL), F32)], axis=0)
    mod = _adaln(cvec, ada_w, ada_b).reshape(DEPTH, 16, 6, D_MODEL)

    w_in_p = _pack_w_in(w_in)
    wb = w_branch.astype(BF16)
    wo = w_out.astype(BF16)
    w1 = mlp_w1.astype(BF16)
    w2 = mlp_w2.astype(BF16)
    fw = final_norm_w.reshape(1, D_MODEL)

    cos, sin = _rope_tables()
    sel = _hgrn_sel()
    cak = cache_attn_k.reshape(DEC_BATCH, DEPTH, PAST_LEN, A_KV_HEADS * HD)
    cav = cache_attn_v.reshape(DEC_BATCH, DEPTH, PAST_LEN, A_KV_HEADS * HD)
    cnk = cache_na_k.reshape(DEC_BATCH, DEPTH, PAST_LEN, BRANCH_WIDTH)
    cnv = cache_na_v.reshape(DEC_BATCH, DEPTH, PAST_LEN, BRANCH_WIDTH)
    lat_blk0 = N_CTX_TOK // DEC_SEQ

    x = jnp.concatenate([x_prompt.reshape(N_CTX_TOK, D_MODEL), x_sample.reshape(N_LAT_TOK, D_MODEL)], axis=0)
    ak_l, av_l, nk_l, nv_l, sd_l, sh_l = [], [], [], [], [], []
    for l in range(DEPTH):
        p = _inproj(x, mod[l], norm_w[l, 0].reshape(1, D_MODEL), w_in_p, l)
        pc = p[:N_CTX_TOK]
        ak_l.append(_pcol(pc, "a_k").reshape(BATCH, SEQ, A_KV_HEADS, HD))
        av_l.append(_pcol(pc, "a_v").reshape(BATCH, SEQ, A_KV_HEADS, HD))
        nk_l.append(_pcol(pc, "d_k").reshape(BATCH, SEQ, D_HEADS, HD))
        nv_l.append(_pcol(pc, "d_v").reshape(BATCH, SEQ, D_HEADS, HD))

        ya_c, yd_c = _ctx_attn(p, attn_sink[l])
        ya_l = _win_attn(p, attn_sink[l], cak, cav, l, cos, sin)
        yd_l = _na_attn(p, cnk, cnv, l, _na_bias_table(na_rpb[l]))

        prm, dnw = _delta_params(delta_a_log[l], delta_dt_bias[l], delta_norm_w[l])
        yb_c, sd = _delta(p, delta_conv[l], prm, dnw, None, L=SEQ, n_seq=BATCH, row_block0=0)
        yb_l = _delta(p, delta_conv[l], prm, dnw, state_delta[:, l].reshape(DEC_BATCH, N_HD, HD, HD),
                      L=DEC_SEQ, n_seq=DEC_BATCH, row_block0=lat_blk0)
        sd_l.append(sd.reshape(BATCH, 2, B_HEADS, HD, HD))

        lbl = lb[l].reshape(1, 2 * BRANCH_WIDTH)
        hnw = jnp.tile(hgrn_norm_w[l].astype(F32), C_HEADS).reshape(1, BRANCH_WIDTH)
        yc_c, sh = _hgrn(p, lbl, hnw, sel, None, L=SEQ, n_seq=BATCH, row_block0=0)
        yc_l = _hgrn(p, lbl, hnw, sel, _hgrn_state_in(state_hgrn[:, l]),
                     L=DEC_SEQ, n_seq=DEC_BATCH, row_block0=lat_blk0)
        sh_l.append(_hgrn_state_out(sh))

        x = _merge(x, [(ya_c, ya_l), (yb_c, yb_l), (yc_c, yc_l), (yd_c, yd_l)], p, mod[l], wb[l], wo[l])
        x = _mlp(x, mod[l], norm_w[l, 1].reshape(1, D_MODEL), w1[l], w2[l], fw, final=(l == DEPTH - 1))

    y_prompt = x[:N_CTX_TOK].reshape(BATCH, SEQ, D_MODEL)
    y_sample = x[N_CTX_TOK:].reshape(DEC_BATCH, DEC_SEQ, D_MODEL)
    return (y_prompt, y_sample, jnp.stack(ak_l, axis=1), jnp.stack(av_l, axis=1), jnp.stack(nk_l, axis=1),
            jnp.stack(nv_l, axis=1), jnp.stack(sd_l, axis=1), jnp.stack(sh_l, axis=1))
```

```python
import functools
import math

import jax
import jax.numpy as jnp
import numpy as np
from jax import lax
from jax.experimental import pallas as pl
from jax.experimental.pallas import tpu as pltpu

F32 = jnp.float32
BF16 = jnp.bfloat16

D_MODEL = 1024
BATCH = 16
SEQ = 256
DEPTH = 4
DEC_BATCH = 8
DEC_SEQ = 1024
PAST_LEN = 512
GRID_W = 64
HEAD_DIM = 64
BRANCH_WIDTH = 512
N_BRANCH = 4
A_HEADS = 8
A_KV_HEADS = 2
A_GROUP = 4
A_WINDOW = 128
A_BLOCK = 128
Q_BLOCK = 128
B_HEADS = 8
DELTA_CHUNK = 64
CONV_K = 5
C_HEADS = 8
HGRN_CHUNK = 16
D_HEADS = 8
NH_ROWS = 8
NH_COLS = 16
NH_QCOLS = 16
NH_KCOLS = 32
D_FF = 4 * D_MODEL
ROPE_BASE = 10000.0
ATTN_SCALE = HEAD_DIM ** -0.5
EPS = 1e-6
NEG = -1e30

N_CTX_TOK = BATCH * SEQ
N_LAT_TOK = DEC_BATCH * DEC_SEQ
N_TOK = N_CTX_TOK + N_LAT_TOK
N_MOD_ROWS = 1 + DEC_BATCH

_REF_COLS = (("a_q", 512), ("a_k", 128), ("a_v", 128), ("b_q", 512), ("b_k", 512), ("b_v", 512),
             ("b_z", 512), ("b_ab", 32), ("c_q", 512), ("c_f", 1024), ("c_i", 512), ("c_g", 512),
             ("d_q", 512), ("d_k", 512), ("d_v", 512), ("g", 4096))
_P_ORDER = ("g", "a_q", "b_q", "b_k", "b_v", "b_z", "c_q", "c_f", "c_i", "c_g", "d_q", "d_k", "d_v",
            "a_k", "a_v", "b_ab")
LANE = 128
N_P = 11264


def _layout():
    ref_off, o = {}, 0
    for name, w in _REF_COLS:
        ref_off[name] = (o, w)
        o += w
    p_off, o = {}, 0
    for name in _P_ORDER:
        w = ref_off[name][1]
        p_off[name] = (o, w)
        o += -(-w // LANE) * LANE
    assert o <= N_P
    return ref_off, p_off


REF_OFF, P_OFF = _layout()

VMEM_LIMIT = 56 * 1024 * 1024


def _mod_row(i, tm):
    nct = N_CTX_TOK // tm
    tpl = DEC_SEQ // tm
    return jnp.where(i < nct, 0, 1 + (i - nct) // tpl)


def _adaln_kernel(c_ref, w_ref, b_ref, o_ref):
    c = c_ref[...]
    s = c * jax.nn.sigmoid(c)
    o_ref[0] = jnp.dot(s, w_ref[0], preferred_element_type=F32) + b_ref[0]


def _adaln(cvec, ada_w, ada_b):
    tn = 1536
    n6 = 6 * D_MODEL
    rows = cvec.shape[0]
    return pl.pallas_call(
        _adaln_kernel,
        grid=(DEPTH, n6 // tn),
        in_specs=[pl.BlockSpec((rows, D_MODEL), lambda l, j: (0, 0)),
                  pl.BlockSpec((1, D_MODEL, tn), lambda l, j: (l, 0, j)),
                  pl.BlockSpec((1, 1, tn), lambda l, j: (l, 0, j))],
        out_specs=pl.BlockSpec((1, rows, tn), lambda l, j: (l, 0, j)),
        out_shape=jax.ShapeDtypeStruct((DEPTH, rows, n6), F32),
        compiler_params=pltpu.CompilerParams(dimension_semantics=("arbitrary", "arbitrary"),
                                             vmem_limit_bytes=VMEM_LIMIT),
        name="adaln",
    )(cvec, ada_w, ada_b.reshape(DEPTH, 1, n6))


ROW_CHUNK = 128


def _norm_mod_to(h_ref, x_ref, nw_ref, shift, scale):
    n = x_ref.shape[0] // ROW_CHUNK

    def body(r, carry):
        rows = pl.ds(pl.multiple_of(r * ROW_CHUNK, ROW_CHUNK), ROW_CHUNK)
        x = x_ref[rows, :]
        y = x * lax.rsqrt(jnp.mean(x * x, axis=-1, keepdims=True) + EPS) * nw_ref[...]
        h_ref[rows, :] = (y * (1.0 + scale) + shift).astype(BF16)
        return carry

    lax.fori_loop(0, n, body, 0)


def _inproj_kernel(x_ref, mod_ref, nw_ref, w_ref, o_ref, h_ref):
    @pl.when(pl.program_id(1) == 0)
    def _():
        _norm_mod_to(h_ref, x_ref, nw_ref, mod_ref[0, 0:1, :], mod_ref[0, 1:2, :])

    o_ref[...] = jnp.dot(h_ref[...], w_ref[0], preferred_element_type=F32)


def _inproj(x, mod, nw, w, layer):
    tm, tn = 1024, N_P // 4
    return pl.pallas_call(
        _inproj_kernel,
        grid=(N_TOK // tm, N_P // tn),
        in_specs=[pl.BlockSpec((tm, D_MODEL), lambda i, j: (i, 0)),
                  pl.BlockSpec((1, 6, D_MODEL), lambda i, j: (_mod_row(i, tm), 0, 0)),
                  pl.BlockSpec((1, D_MODEL), lambda i, j: (0, 0)),
                  pl.BlockSpec((1, D_MODEL, tn), lambda i, j: (layer, 0, j))],
        out_specs=pl.BlockSpec((tm, tn), lambda i, j: (i, j)),
        out_shape=jax.ShapeDtypeStruct((N_TOK, N_P), F32),
        scratch_shapes=[pltpu.VMEM((tm, D_MODEL), BF16)],
        compiler_params=pltpu.CompilerParams(dimension_semantics=("arbitrary", "arbitrary"),
                                             vmem_limit_bytes=VMEM_LIMIT),
        name="inproj",
    )(x, mod, nw, w)


MERGE_TM = 256
MERGE_CTX_TILES = N_CTX_TOK // MERGE_TM


def _merge_kernel(x_ref, *refs):
    y_refs, (g_ref, mod_ref, wb_ref, wo_ref, o_ref) = refs[:2 * N_BRANCH], refs[2 * N_BRANCH:]
    is_ctx = pl.program_id(0) < MERGE_CTX_TILES
    merged = None
    for k in range(N_BRANCH):
        y = jnp.where(is_ctx, y_refs[2 * k][...], y_refs[2 * k + 1][...])
        yp = jnp.dot(y.astype(BF16), wb_ref[k], preferred_element_type=F32)
        t = jax.nn.sigmoid(g_ref[:, k * D_MODEL:(k + 1) * D_MODEL]) * yp
        merged = t if merged is None else merged + t
    o = jnp.dot(merged.astype(BF16), wo_ref[...], preferred_element_type=F32)
    o_ref[...] = x_ref[...] + mod_ref[0, 2:3, :] * o


def _merge(x, ys, p, mod, wb, wo):
    tm = MERGE_TM
    nct = MERGE_CTX_TILES
    cspec = pl.BlockSpec((tm, BRANCH_WIDTH), lambda i: (jnp.minimum(i, nct - 1), 0))
    lspec = pl.BlockSpec((tm, BRANCH_WIDTH), lambda i: (jnp.maximum(i - nct, 0), 0))
    return pl.pallas_call(
        _merge_kernel,
        grid=(N_TOK // tm,),
        in_specs=[pl.BlockSpec((tm, D_MODEL), lambda i: (i, 0))]
        + [cspec, lspec] * N_BRANCH
        + [pl.BlockSpec((tm, N_BRANCH * D_MODEL), lambda i: (i, 0)),
           pl.BlockSpec((1, 6, D_MODEL), lambda i: (_mod_row(i, tm), 0, 0)),
           pl.BlockSpec((N_BRANCH, BRANCH_WIDTH, D_MODEL), lambda i: (0, 0, 0)),
           pl.BlockSpec((D_MODEL, D_MODEL), lambda i: (0, 0))],
        out_specs=pl.BlockSpec((tm, D_MODEL), lambda i: (i, 0)),
        out_shape=jax.ShapeDtypeStruct((N_TOK, D_MODEL), F32),
        compiler_params=pltpu.CompilerParams(dimension_semantics=("arbitrary",),
                                             vmem_limit_bytes=VMEM_LIMIT),
        name="merge",
    )(x, *[y for pair in ys for y in pair], p, mod, wb, wo)


def _mlp_kernel(x_ref, mod_ref, nw_ref, w1_ref, w2_ref, fw_ref, o_ref, h_ref, acc_ref, *, final):
    f = pl.program_id(1)

    @pl.when(f == 0)
    def _():
        _norm_mod_to(h_ref, x_ref, nw_ref, mod_ref[0, 3:4, :], mod_ref[0, 4:5, :])

    a = jnp.dot(h_ref[...], w1_ref[...], preferred_element_type=F32)
    a = jnp.square(jnp.maximum(a, 0.0)).astype(BF16)
    contrib = jnp.dot(a, w2_ref[...], preferred_element_type=F32)

    @pl.when(f == 0)
    def _():
        acc_ref[...] = contrib

    @pl.when(f != 0)
    def _():
        acc_ref[...] += contrib

    @pl.when(f == pl.num_programs(1) - 1)
    def _():
        y = x_ref[...] + mod_ref[0, 5:6, :] * acc_ref[...]
        if final:
            y = y * lax.rsqrt(jnp.mean(y * y, axis=-1, keepdims=True) + EPS) * fw_ref[...]
        o_ref[...] = y


def _mlp(x, mod, nw, w1, w2, fw, final):
    tm, tf = 1024, 1024
    return pl.pallas_call(
        functools.partial(_mlp_kernel, final=final),
        grid=(N_TOK // tm, D_FF // tf),
        in_specs=[pl.BlockSpec((tm, D_MODEL), lambda i, f: (i, 0)),
                  pl.BlockSpec((1, 6, D_MODEL), lambda i, f: (_mod_row(i, tm), 0, 0)),
                  pl.BlockSpec((1, D_MODEL), lambda i, f: (0, 0)),
                  pl.BlockSpec((D_MODEL, tf), lambda i, f: (0, f)),
                  pl.BlockSpec((tf, D_MODEL), lambda i, f: (f, 0)),
                  pl.BlockSpec((1, D_MODEL), lambda i, f: (0, 0))],
        out_specs=pl.BlockSpec((tm, D_MODEL), lambda i, f: (i, 0)),
        out_shape=jax.ShapeDtypeStruct((N_TOK, D_MODEL), F32),
        scratch_shapes=[pltpu.VMEM((tm, D_MODEL), BF16), pltpu.VMEM((tm, D_MODEL), F32)],
        compiler_params=pltpu.CompilerParams(dimension_semantics=("arbitrary", "arbitrary"),
                                             vmem_limit_bytes=VMEM_LIMIT),
        name="mlp",
    )(x, mod, nw, w1, w2, fw)


HD = HEAD_DIM
N_HD = 2 * B_HEADS
CONV_PAD = 8


def _dot_nt(a, b):
    return lax.dot_general(a, b, (((1,), (1,)), ((), ())), preferred_element_type=F32)


def _dot_tn(a, b):
    return lax.dot_general(a, b, (((0,), (0,)), ((), ())), preferred_element_type=F32)


def _dot(a, b):
    return jnp.dot(a, b, preferred_element_type=F32)


def _split(x):
    hi = x.astype(BF16)
    return hi, (x - hi.astype(F32)).astype(BF16)


def _dot3(a, b):
    return _dot(a[0], b[0]) + (_dot(a[0], b[1]) + _dot(a[1], b[0]))


def _pair_mean_sq(x2, first_head):
    sq = x2 * x2
    s0 = jnp.sum(jnp.where(first_head, sq, 0.0), axis=-1, keepdims=True)
    s1 = jnp.sum(jnp.where(first_head, 0.0, sq), axis=-1, keepdims=True)
    return jnp.where(first_head, s0, s1) * (1.0 / HD)


def _softplus(x):
    return jnp.maximum(x, 0.0) + jnp.log1p(jnp.exp(-jnp.abs(x)))


def _delta_kernel(*refs, L, has_s0):
    if has_s0:
        (q_ref, k_ref, v_ref, z_ref, ab_ref, cw_ref, prm_ref, nw_ref, s0_ref, y_ref,
         xpad, qkv_s, g_s, b_s, o_s, st_s) = refs
        sout_ref = None
    else:
        (q_ref, k_ref, v_ref, z_ref, ab_ref, cw_ref, prm_ref, nw_ref, y_ref, sout_ref,
         xpad, qkv_s, g_s, b_s, o_s, st_s) = refs
        s0_ref = None
    C = DELTA_CHUNK
    n_chunks = L // C

    first_head = lax.broadcasted_iota(jnp.int32, (C, LANE), 1) < HD
    zeros_pad = jnp.zeros((CONV_PAD, BRANCH_WIDTH), F32)
    xpad[0:CONV_PAD, :] = zeros_pad
    xpad[CONV_PAD + L:2 * CONV_PAD + L, :] = zeros_pad
    for idx, src in enumerate((q_ref, k_ref, v_ref)):
        xpad[CONV_PAD:CONV_PAD + L, :] = src[...]
        for r in range(n_chunks):
            acc = None
            for j in range(CONV_K):
                start = CONV_PAD + r * C + j - CONV_K // 2
                t = xpad[start:start + C, :] * cw_ref[j:j + 1, idx * BRANCH_WIDTH:(idx + 1) * BRANCH_WIDTH]
                acc = t if acc is None else acc + t
            y = acc * jax.nn.sigmoid(acc)
            if idx == 2:
                qkv_s[idx, r * C:(r + 1) * C, :] = y
            else:
                for c in range(BRANCH_WIDTH // LANE):
                    y2 = y[:, c * LANE:(c + 1) * LANE]
                    sq = y2 * y2
                    s0 = jnp.sum(jnp.where(first_head, sq, 0.0), axis=-1, keepdims=True)
                    s1 = jnp.sum(jnp.where(first_head, 0.0, sq), axis=-1, keepdims=True)
                    inv = lax.rsqrt(jnp.where(first_head, s0, s1) + EPS)
                    if idx == 0:
                        inv = inv * ATTN_SCALE
                    qkv_s[idx, r * C:(r + 1) * C, c * LANE:(c + 1) * LANE] = y2 * inv

    ab = ab_ref[...]
    g_s[...] = -jnp.exp(prm_ref[0:1, :]) * _softplus(ab + prm_ref[1:2, :])
    b_s[...] = jax.nn.sigmoid(ab)

    if has_s0:
        st_s[...] = s0_ref[0]
    else:
        st_s[...] = jnp.zeros_like(st_s)

    ri = lax.broadcasted_iota(jnp.int32, (C, C), 0)
    ci = lax.broadcasted_iota(jnp.int32, (C, C), 1)
    eye = (ri == ci).astype(F32)
    level_masks = [(ri // 2) == (ci // 2)]
    blk = 2
    while blk < C:
        level_masks.append(((ri // (2 * blk)) == (ci // (2 * blk))) & ((ri // blk) != (ci // blk)))
        blk *= 2

    def body(n, carry):
        P = []
        for d in range(2):
            chunk = n if d == 0 else n_chunks - 1 - n
            rows = pl.ds(pl.multiple_of(chunk * C, C), C)
            incl = (ri >= ci) if d == 0 else (ri <= ci)
            strict = (ri > ci) if d == 0 else (ri < ci)
            g = g_s[rows, :]
            beta = b_s[rows, :]
            gc = jnp.dot(incl.astype(F32), g, precision=lax.Precision.HIGHEST, preferred_element_type=F32)
            gct = gc.T
            tot = gc[C - 1:C, :] if d == 0 else gc[0:1, :]
            eg = jnp.exp(gc)
            ek = jnp.exp(tot - gc)
            etot = jnp.exp(tot)
            for h in range(B_HEADS):
                c = d * B_HEADS + h
                hs = slice(h * HD, (h + 1) * HD)
                P.append(dict(d=d, c=c, hs=hs, rows=rows, incl=incl, strict=strict,
                              qh=qkv_s[0, rows, hs], kh=qkv_s[1, rows, hs], vh=qkv_s[2, rows, hs],
                              bcol=beta[:, N_HD + c:N_HD + c + 1], gcol=gc[:, c:c + 1], grow=gct[c:c + 1, :],
                              egc=eg[:, c:c + 1], ekc=ek[:, c:c + 1], etc=etot[:, c:c + 1]))
        for p in P:
            p["qk"] = _dot_nt(jnp.concatenate([p["qh"], p["kh"]], axis=0), p["kh"])
        for p in P:
            decay = jnp.exp(jnp.where(p["incl"], p["gcol"] - p["grow"], NEG))
            p["pqk"] = p["qk"][:C] * decay
            p["amat"] = jnp.where(p["strict"], p["bcol"] * p["qk"][C:] * decay, 0.0)
            p["rhs"] = jnp.concatenate([p["bcol"] * p["vh"], (p["bcol"] * p["egc"]) * p["kh"]], axis=1)
            p["tinv"] = eye - jnp.where(level_masks[0], p["amat"], 0.0)
        for lm in level_masks[1:]:
            for p in P:
                p["ts"] = _split(p["tinv"])
                p["et"] = _dot3(_split(jnp.where(lm, p["amat"], 0.0)), p["ts"])
            for p in P:
                p["tinv"] = p["tinv"] - _dot3(p["ts"], _split(p["et"]))
        for p in P:
            p["sol"] = _dot3(_split(p["tinv"]), _split(p["rhs"]))
        for p in P:
            p["s"] = st_s[p["c"]]
            p["t"] = _dot(jnp.concatenate([p["qh"] * p["egc"], p["sol"][:, HD:]], axis=0), p["s"])
        for p in P:
            p["u"] = p["sol"][:, :HD] - p["t"][C:]
            p["o"] = p["t"][:C] + _dot(p["pqk"], p["u"])
            p["s_new"] = p["etc"] * p["s"] + _dot_tn(p["kh"] * p["ekc"], p["u"])
        for p in P:
            o_s[p["d"], p["rows"], p["hs"]] = p["o"]
            st_s[p["c"]] = p["s_new"]
        return carry

    lax.fori_loop(0, n_chunks, body, 0)

    if not has_s0:
        sout_ref[0] = st_s[...]

    def out_body(r, carry):
        rows = pl.ds(pl.multiple_of(r * C, C), C)
        z = z_ref[rows, :]
        gate = z * jax.nn.sigmoid(z) * nw_ref[...]
        for c in range(BRANCH_WIDTH // LANE):
            cs = slice(c * LANE, (c + 1) * LANE)
            o = o_s[0, rows, cs] + o_s[1, rows, cs]
            y_ref[rows, cs] = o * lax.rsqrt(_pair_mean_sq(o, first_head) + EPS) * gate[:, cs]
        return carry

    lax.fori_loop(0, n_chunks, out_body, 0)


def _delta(p, cw, prm, nw, s0, *, L, n_seq, row_block0):
    has_s0 = s0 is not None

    in_specs = [_pspec(n, L, row_block0) for n in ("b_q", "b_k", "b_v", "b_z", "b_ab")] + [
                pl.BlockSpec((CONV_K, 3 * BRANCH_WIDTH), lambda i: (0, 0)),
                pl.BlockSpec((8, LANE), lambda i: (0, 0)),
                pl.BlockSpec((1, BRANCH_WIDTH), lambda i: (0, 0))]
    args = [p, p, p, p, p, cw, prm, nw]
    st_spec = pl.BlockSpec((1, N_HD, HD, HD), lambda i: (i, 0, 0, 0))
    y_shape = jax.ShapeDtypeStruct((n_seq * L, BRANCH_WIDTH), F32)
    y_spec = pl.BlockSpec((L, BRANCH_WIDTH), lambda i: (i, 0))
    if has_s0:
        in_specs.append(st_spec)
        args.append(s0)
        out_specs, out_shape = y_spec, y_shape
    else:
        out_specs = (y_spec, st_spec)
        out_shape = (y_shape, jax.ShapeDtypeStruct((n_seq, N_HD, HD, HD), F32))
    return pl.pallas_call(
        functools.partial(_delta_kernel, L=L, has_s0=has_s0),
        grid=(n_seq,),
        in_specs=in_specs,
        out_specs=out_specs,
        out_shape=out_shape,
        scratch_shapes=[pltpu.VMEM((L + 2 * CONV_PAD, BRANCH_WIDTH), F32),
                        pltpu.VMEM((3, L, BRANCH_WIDTH), F32),
                        pltpu.VMEM((L, LANE), F32),
                        pltpu.VMEM((L, LANE), F32),
                        pltpu.VMEM((2, L, BRANCH_WIDTH), F32),
                        pltpu.VMEM((N_HD, HD, HD), F32)],
        compiler_params=pltpu.CompilerParams(dimension_semantics=("arbitrary",),
                                             vmem_limit_bytes=VMEM_LIMIT),
        name="delta_lat" if has_s0 else "delta_ctx",
    )(*args)


def _delta_params(a_log, dt_bias, norm_w):
    prm = jnp.zeros((8, LANE), F32)
    prm = prm.at[0, :N_HD].set(a_log.reshape(N_HD).astype(F32))
    prm = prm.at[1, :N_HD].set(dt_bias.reshape(N_HD).astype(F32))
    return prm, jnp.tile(norm_w.astype(F32), B_HEADS).reshape(1, BRANCH_WIDTH)


def _attend(problems):
    def scores(p):
        p["s"] = [(_dot_nt(p["q"], k) if b is None else _dot_nt(p["q"], k) + b)
                  for k, b in zip(p["ks"], p["biases"])]

    def softmax(p):
        m = None
        for s in p["s"]:
            mi = jnp.max(s, axis=-1, keepdims=True)
            m = mi if m is None else jnp.maximum(m, mi)
        if p["sink"] is not None:
            m = jnp.maximum(m, p["sink"])
        p["p"] = [jnp.exp(s - m) for s in p["s"]]
        den = None
        for pr in p["p"]:
            di = jnp.sum(pr, axis=-1, keepdims=True)
            den = di if den is None else den + di
        if p["sink"] is not None:
            den = den + jnp.exp(p["sink"] - m)
        p["den"] = den

    def values(p):
        acc = None
        for pr, v in zip(p["p"], p["vs"]):
            ai = _dot(pr.astype(v.dtype), v)
            acc = ai if acc is None else acc + ai
        return acc / p["den"]

    n = len(problems)
    outs = []
    for t in range(n + 2):
        if t < n:
            scores(problems[t])
        if 1 <= t <= n:
            softmax(problems[t - 1])
        if t >= 2:
            outs.append(values(problems[t - 2]))
    return outs


def _ctx_attn_kernel(sink_ref, aq_ref, ak_ref, av_ref, dq_ref, dk_ref, dv_ref, ya_ref, yd_ref):
    L = aq_ref.shape[0]
    lane = lax.broadcasted_iota(jnp.int32, (L, LANE), 1)
    n_tiles = BRANCH_WIDTH // LANE
    problems = []
    akb = ak_ref[...].astype(BF16)
    avb = av_ref[...].astype(BF16)
    for c in range(n_tiles):
        q2 = aq_ref[:, c * LANE:(c + 1) * LANE] * ATTN_SCALE
        for e in range(2):
            h = 2 * c + e
            j = h // A_GROUP
            t = q2 if e == j else pltpu.roll(q2, HD, 1)
            problems.append(dict(q=jnp.where((lane // HD) == j, t, 0.0).astype(BF16), ks=[akb], vs=[avb],
                                 biases=[None], sink=jnp.full((L, 1), sink_ref[h], F32)))
    for c in range(n_tiles):
        cs = slice(c * LANE, (c + 1) * LANE)
        problems.append(dict(q=_pair_stack(dq_ref[:, cs] * ATTN_SCALE).astype(BF16),
                             ks=[dk_ref[:, cs].astype(BF16)], vs=[dv_ref[:, cs].astype(BF16)],
                             biases=[None], sink=None))
    outs = _attend(problems)
    for c in range(n_tiles):
        halves = []
        for e in range(2):
            h = 2 * c + e
            halves.append(outs[h] if e == h // A_GROUP else pltpu.roll(outs[h], HD, 1))
        ya_ref[:, c * LANE:(c + 1) * LANE] = jnp.where(lane < HD, halves[0], halves[1])
        yd_ref[:, c * LANE:(c + 1) * LANE] = _pair_unstack(outs[A_HEADS + c])


def _pspec(name, rows, row_block0):
    off, w = P_OFF[name]
    bw = max(w, LANE)
    return pl.BlockSpec((rows, bw), lambda i: (row_block0 + i, off // bw))


_SMEM_SPEC = pl.BlockSpec(memory_space=pltpu.SMEM)


def _ctx_attn(p, sink):
    y_shape = jax.ShapeDtypeStruct((N_CTX_TOK, BRANCH_WIDTH), F32)
    y_spec = pl.BlockSpec((SEQ, BRANCH_WIDTH), lambda i: (i, 0))
    return pl.pallas_call(
        _ctx_attn_kernel,
        grid=(BATCH,),
        in_specs=[_SMEM_SPEC] + [_pspec(n, SEQ, 0) for n in ("a_q", "a_k", "a_v", "d_q", "d_k", "d_v")],
        out_specs=(y_spec, y_spec),
        out_shape=(y_shape, y_shape),
        compiler_params=pltpu.CompilerParams(dimension_semantics=("arbitrary",),
                                             vmem_limit_bytes=VMEM_LIMIT),
        name="ctx_attn",
    )(sink, p, p, p, p, p, p)


def _rope_tables():
    t = np.arange(DEC_SEQ)
    quarter = HD // 4
    inv = ROPE_BASE ** (-np.arange(quarter, dtype=np.float64) / quarter)
    ang_r = (t // GRID_W)[:, None] * inv[None, :]
    ang_c = (t % GRID_W)[:, None] * inv[None, :]
    cos = np.concatenate([np.cos(ang_r)] * 2 + [np.cos(ang_c)] * 2, axis=1)
    sin = np.concatenate([-np.sin(ang_r), np.sin(ang_r), -np.sin(ang_c), np.sin(ang_c)], axis=1)
    return (jnp.asarray(np.tile(cos, (1, 2)), F32), jnp.asarray(np.tile(sin, (1, 2)), F32))


def _rope128(x, cos, sin):
    lane = lax.broadcasted_iota(jnp.int32, x.shape, 1)
    swapped = jnp.where((lane % 32) < 16, pltpu.roll(x, LANE - 16, 1), pltpu.roll(x, 16, 1))
    return x * cos + swapped * sin


def _win_attn_kernel(sink_ref, q_ref, k_ref, v_ref, ck_ref, cv_ref, cos_ref, sin_ref, y_ref,
                     qst_s, kr_s, vb_s, ckb_s, cvb_s):
    L = DEC_SEQ
    nb = L // A_BLOCK
    cos = cos_ref[...]
    sin = sin_ref[...]
    lane = lax.broadcasted_iota(jnp.int32, (L, LANE), 1)
    kr_s[...] = _rope128(k_ref[...], cos, sin).astype(BF16)
    vb_s[...] = v_ref[...].astype(BF16)
    ckb_s[...] = ck_ref[0, 0].astype(BF16)
    cvb_s[...] = cv_ref[0, 0].astype(BF16)
    for c in range(BRANCH_WIDTH // LANE):
        qr = _rope128(q_ref[:, c * LANE:(c + 1) * LANE], cos, sin) * ATTN_SCALE
        for e in range(2):
            h = 2 * c + e
            j = h // A_GROUP
            t = qr if e == j else pltpu.roll(qr, HD, 1)
            qst_s[h] = jnp.where((lane // HD) == j, t, 0.0).astype(BF16)

    W = 3 * A_BLOCK
    qi = lax.broadcasted_iota(jnp.int32, (A_BLOCK, W), 0)
    ki = lax.broadcasted_iota(jnp.int32, (A_BLOCK, W), 1)
    lane_b = lax.broadcasted_iota(jnp.int32, (A_BLOCK, LANE), 1)

    def body(i, carry):
        start = jnp.clip(i - 1, 0, nb - 3) * A_BLOCK
        rows = pl.ds(pl.multiple_of(i * A_BLOCK, A_BLOCK), A_BLOCK)
        krows = pl.ds(pl.multiple_of(start, A_BLOCK), W)
        ok = jnp.abs(i * A_BLOCK + qi - (start + ki)) <= A_WINDOW
        bias = jnp.where(ok, 0.0, NEG)
        problems = []
        for h in range(A_HEADS):
            problems.append(dict(q=qst_s[h, rows, :], ks=[kr_s[krows, :], ckb_s[...]],
                                 vs=[vb_s[krows, :], cvb_s[...]], biases=[bias, None],
                                 sink=jnp.full((A_BLOCK, 1), sink_ref[h], F32)))
        outs = _attend(problems)
        for c in range(BRANCH_WIDTH // LANE):
            halves = []
            for e in range(2):
                h = 2 * c + e
                halves.append(outs[h] if e == h // A_GROUP else pltpu.roll(outs[h], HD, 1))
            y_ref[rows, c * LANE:(c + 1) * LANE] = jnp.where(lane_b < HD, halves[0], halves[1])
        return carry

    lax.fori_loop(0, nb, body, 0)


def _win_attn(p, sink, cache_k, cache_v, layer, cos, sin):
    row0 = N_CTX_TOK // DEC_SEQ
    cspec = pl.BlockSpec((1, 1, PAST_LEN, LANE), lambda i: (i, layer, 0, 0))
    tspec = pl.BlockSpec((DEC_SEQ, LANE), lambda i: (0, 0))
    return pl.pallas_call(
        _win_attn_kernel,
        grid=(DEC_BATCH,),
        in_specs=[_SMEM_SPEC] + [_pspec(n, DEC_SEQ, row0) for n in ("a_q", "a_k", "a_v")]
        + [cspec, cspec, tspec, tspec],
        out_specs=pl.BlockSpec((DEC_SEQ, BRANCH_WIDTH), lambda i: (i, 0)),
        out_shape=jax.ShapeDtypeStruct((N_LAT_TOK, BRANCH_WIDTH), F32),
        scratch_shapes=[pltpu.VMEM((A_HEADS, DEC_SEQ, LANE), BF16), pltpu.VMEM((DEC_SEQ, LANE), BF16),
                        pltpu.VMEM((DEC_SEQ, LANE), BF16), pltpu.VMEM((PAST_LEN, LANE), BF16),
                        pltpu.VMEM((PAST_LEN, LANE), BF16)],
        compiler_params=pltpu.CompilerParams(dimension_semantics=("arbitrary",),
                                             vmem_limit_bytes=VMEM_LIMIT),
        name="win_attn",
    )(sink, p, p, p, cache_k, cache_v, cos, sin)


N_GRID_ROWS = DEC_SEQ // GRID_W
N_DR = 2 * NH_ROWS - 1


def _na_bias_table(rpb):
    qc = np.arange(GRID_W)[:, None]
    kc = np.arange(GRID_W)[None, :]
    wstart = np.clip(qc - NH_COLS // 2, 0, GRID_W - NH_COLS)
    ok = (kc >= wstart) & (kc < wstart + NH_COLS)
    dc = np.clip(kc - qc + NH_COLS - 1, 0, 2 * NH_COLS - 2)
    t = jnp.where(ok[None, None], rpb.astype(F32)[:, :, dc], NEG)
    return jnp.concatenate([t[:, :-1], t[:, 1:]], axis=-1)


NA_ROWS_PER_STEP = 2


def _pair_stack(q2):
    lane = lax.broadcasted_iota(jnp.int32, q2.shape, 1)
    return jnp.concatenate([jnp.where(lane < HD, q2, 0.0), jnp.where(lane >= HD, q2, 0.0)], axis=0)


def _pair_unstack(o):
    m = o.shape[0] // 2
    lane = lax.broadcasted_iota(jnp.int32, (m, LANE), 1)
    return jnp.where(lane < HD, o[:m], o[m:])


def _na_attn_kernel(q_ref, k_ref, v_ref, ck_ref, cv_ref, t_ref, y_ref, kb_s, vb_s, ckb_s, cvb_s):
    kh = NH_ROWS
    n_loc = kh * GRID_W
    n_pair = D_HEADS // 2
    kb_s[...] = k_ref[...].astype(BF16)
    vb_s[...] = v_ref[...].astype(BF16)
    ckb_s[...] = ck_ref[0, 0].astype(BF16)
    cvb_s[...] = cv_ref[0, 0].astype(BF16)

    def body(i, carry):
        problems = []
        row_sl = []
        for rr in range(NA_ROWS_PER_STEP):
            r = i * NA_ROWS_PER_STEP + rr
            rs = jnp.clip(r - kh // 2, 0, N_GRID_ROWS - kh)
            rows = pl.ds(pl.multiple_of(r * GRID_W, GRID_W), GRID_W)
            krows = pl.ds(pl.multiple_of(rs * GRID_W, GRID_W), n_loc)
            s0 = rs - r + NH_ROWS - 1
            row_sl.append(rows)
            for hp in range(n_pair):
                ps = slice(hp * LANE, (hp + 1) * LANE)
                bias = jnp.concatenate(
                    [jnp.concatenate([t_ref[2 * hp + e, s0 + 2 * w] for w in range(kh // 2)], axis=1)
                     for e in range(2)], axis=0)
                q = _pair_stack(q_ref[rows, ps] * ATTN_SCALE).astype(BF16)
                problems.append(dict(q=q, ks=[kb_s[krows, ps], ckb_s[:, ps]],
                                     vs=[vb_s[krows, ps], cvb_s[:, ps]], biases=[bias, None], sink=None))
        outs = _attend(problems)
        for rr in range(NA_ROWS_PER_STEP):
            for hp in range(n_pair):
                y_ref[row_sl[rr], hp * LANE:(hp + 1) * LANE] = _pair_unstack(outs[rr * n_pair + hp])
        return carry

    lax.fori_loop(0, N_GRID_ROWS // NA_ROWS_PER_STEP, body, 0)


def _na_attn(p, cache_k, cache_v, layer, table):
    row0 = N_CTX_TOK // DEC_SEQ
    cspec = pl.BlockSpec((1, 1, PAST_LEN, BRANCH_WIDTH), lambda i: (i, layer, 0, 0))
    return pl.pallas_call(
        _na_attn_kernel,
        grid=(DEC_BATCH,),
        in_specs=[_pspec(n, DEC_SEQ, row0) for n in ("d_q", "d_k", "d_v")]
        + [cspec, cspec, pl.BlockSpec((D_HEADS, N_DR - 1, GRID_W, LANE), lambda i: (0, 0, 0, 0))],
        out_specs=pl.BlockSpec((DEC_SEQ, BRANCH_WIDTH), lambda i: (i, 0)),
        out_shape=jax.ShapeDtypeStruct((N_LAT_TOK, BRANCH_WIDTH), F32),
        scratch_shapes=[pltpu.VMEM((DEC_SEQ, BRANCH_WIDTH), BF16), pltpu.VMEM((DEC_SEQ, BRANCH_WIDTH), BF16),
                        pltpu.VMEM((PAST_LEN, BRANCH_WIDTH), BF16), pltpu.VMEM((PAST_LEN, BRANCH_WIDTH), BF16)],
        compiler_params=pltpu.CompilerParams(dimension_semantics=("arbitrary",),
                                             vmem_limit_bytes=VMEM_LIMIT),
        name="na_attn",
    )(p, p, p, cache_k, cache_v, table)


HG = 128
N_PAIR = C_HEADS // 2
CPG = HG // HGRN_CHUNK
SUB = 8
LOG2_E = 1.4426950408889634


def _hgrn_sel():
    j = np.arange(HGRN_CHUNK)[:, None, None, None]
    lane = np.arange(LANE)[None, :, None, None]
    e = np.arange(2)[None, None, :, None]
    c = np.arange(LANE)[None, None, None, :]
    sel = ((lane // HD) == e) & ((c % HGRN_CHUNK) == j)
    return jnp.asarray(sel.reshape(HGRN_CHUNK * LANE, 2 * LANE), BF16)


def _hgrn_kernel(*refs, L, has_s0):
    if has_s0:
        (q_ref, f_ref, i_ref, g_ref, lb_ref, nw_ref, sel_ref, s0_ref, y_ref,
         lf_s, ck_s, qs_s, o_s, st_s, zc_s) = refs
        sout_ref = None
    else:
        (q_ref, f_ref, i_ref, g_ref, lb_ref, nw_ref, sel_ref, y_ref, sout_ref,
         lf_s, ck_s, qs_s, o_s, st_s, zc_s) = refs
        s0_ref = None
    C = HGRN_CHUNK
    n_groups = L // HG
    R = 64

    lb = lb_ref[...]
    log_lb = jnp.log(lb)
    log_1mlb = jnp.log1p(-lb)

    def pre_body(r, carry):
        rows = pl.ds(pl.multiple_of(r * R, R), R)
        cf = f_ref[rows, :]
        b = log_1mlb - _softplus(-cf)
        lf_s[rows, :] = jnp.maximum(log_lb, b) + jnp.log1p(jnp.exp(-jnp.abs(log_lb - b)))
        ck_s[rows, :] = (1.0 - lb) * jax.nn.sigmoid(-cf)
        cq = q_ref[rows, :]
        qs_s[rows, :] = cq * jax.nn.sigmoid(cq)
        return carry

    lax.fori_loop(0, L // R, pre_body, 0)

    if has_s0:
        st_s[...] = s0_ref[0]
    else:
        st_s[...] = jnp.zeros_like(st_s)

    ri = lax.broadcasted_iota(jnp.int32, (HG, HG), 0)
    ci = lax.broadcasted_iota(jnp.int32, (HG, HG), 1)
    same_chunk = (ri // C) == (ci // C)
    same_head = (ri // HD) == (ci // HD)
    tl8 = lax.broadcasted_iota(jnp.int32, (CPG, SUB, LANE), 1)

    cum_mats = [(same_chunk & ((ci <= ri) if d == 0 else (ci >= ri))).astype(F32) for d in range(2)]

    def body(n, carry):
        P = []
        for d in range(2):
            gi = n if d == 0 else n_groups - 1 - n
            rows = pl.ds(pl.multiple_of(gi * HG, HG), HG)
            for hp in range(N_PAIR):
                fcols = slice(d * BRANCH_WIDTH + hp * LANE, d * BRANCH_WIDTH + (hp + 1) * LANE)
                hcols = slice(hp * LANE, (hp + 1) * LANE)
                P.append(dict(d=d, sidx=d * N_PAIR + hp, rows=rows, hcols=hcols,
                              lf=lf_s[rows, fcols], kk=ck_s[rows, fcols], qq=qs_s[rows, hcols],
                              vv=i_ref[rows, hcols]))
        for p in P:
            p["bcum"] = jnp.dot(cum_mats[p["d"]], p["lf"], precision=lax.Precision.HIGHEST,
                                preferred_element_type=F32)
        for p in P:
            fwd = p["d"] == 0
            b4 = (p["bcum"] * LOG2_E).reshape(CPG, 2, SUB, LANE)
            k4 = p["kk"].reshape(CPG, 2, SUB, LANE)
            q4 = p["qq"].reshape(CPG, 2, SUB, LANE)
            for j in range(C):
                jh, jl = divmod(j, SUB)
                bj = jnp.broadcast_to(b4[:, jh, jl:jl + 1, :], (CPG, SUB, LANE))
                kj = jnp.broadcast_to(k4[:, jh, jl:jl + 1, :], (CPG, SUB, LANE))
                halves = []
                for th in range(2):
                    if th == jh:
                        ok = (tl8 >= jl) if fwd else (tl8 <= jl)
                        e = jnp.exp2(jnp.where(ok, b4[:, th] - bj, NEG))
                    elif (th > jh) == fwd:
                        e = jnp.exp2(b4[:, th] - bj)
                    else:
                        halves.append(jnp.zeros((CPG, SUB, LANE), F32))
                        continue
                    halves.append(q4[:, th] * e * kj)
                z = jnp.stack(halves, axis=1).reshape(HG, LANE)
                zc_s[p["sidx"], :, j * LANE:(j + 1) * LANE] = z.astype(BF16)
        for p in P:
            p["att"] = _dot(zc_s[p["sidx"]], sel_ref[...])
        for p in P:
            p["o_intra"] = jnp.concatenate(
                [_dot(jnp.where(same_chunk, p["att"][:, e * LANE:(e + 1) * LANE], 0.0),
                      p["vv"][:, e * HD:(e + 1) * HD]) for e in range(2)], axis=1)
            p["st"] = st_s[p["sidx"]]
            p["o_inter"] = [None] * CPG
        for cix in range(CPG):
            for p in P:
                c = cix if p["d"] == 0 else CPG - 1 - cix
                r16 = slice(c * C, (c + 1) * C)
                bc = p["bcum"][r16]
                blast = bc[C - 1:C] if p["d"] == 0 else bc[0:1]
                p["o_inter"][c] = _pair_unstack(_dot_nt(_pair_stack(p["qq"][r16] * jnp.exp(bc)), p["st"]))
                p["st"] = p["st"] * jnp.exp(blast) + _dot_tn(p["vv"][r16], p["kk"][r16] * jnp.exp(blast - bc))
        for p in P:
            o_s[p["d"], p["rows"], p["hcols"]] = p["o_intra"] + jnp.concatenate(p["o_inter"], axis=0)
            st_s[p["sidx"]] = p["st"]
        return carry

    lax.fori_loop(0, n_groups, body, 0)

    if not has_s0:
        for sidx in range(2 * N_PAIR):
            sout_ref[0, sidx] = jnp.where(same_head, st_s[sidx], 0.0)

    first_head = lax.broadcasted_iota(jnp.int32, (R, LANE), 1) < HD

    def out_body(r, carry):
        rows = pl.ds(pl.multiple_of(r * R, R), R)
        gate = jax.nn.sigmoid(g_ref[rows, :])
        for c in range(BRANCH_WIDTH // LANE):
            cs = slice(c * LANE, (c + 1) * LANE)
            o = (o_s[0, rows, cs] + o_s[1, rows, cs]) * gate[:, cs]
            y_ref[rows, cs] = o * lax.rsqrt(_pair_mean_sq(o, first_head) + EPS) * nw_ref[:, cs]
        return carry

    lax.fori_loop(0, L // R, out_body, 0)


def _hgrn(p, lb, nw, sel, s0, *, L, n_seq, row_block0):
    has_s0 = s0 is not None
    in_specs = [_pspec(n, L, row_block0) for n in ("c_q", "c_f", "c_i", "c_g")] + [
        pl.BlockSpec((1, 2 * BRANCH_WIDTH), lambda i: (0, 0)),
        pl.BlockSpec((1, BRANCH_WIDTH), lambda i: (0, 0)),
        pl.BlockSpec((HGRN_CHUNK * LANE, 2 * LANE), lambda i: (0, 0))]
    args = [p, p, p, p, lb, nw, sel]
    st_spec = pl.BlockSpec((1, 2 * N_PAIR, LANE, LANE), lambda i: (i, 0, 0, 0))
    y_shape = jax.ShapeDtypeStruct((n_seq * L, BRANCH_WIDTH), F32)
    y_spec = pl.BlockSpec((L, BRANCH_WIDTH), lambda i: (i, 0))
    if has_s0:
        in_specs.append(st_spec)
        args.append(s0)
        out_specs, out_shape = y_spec, y_shape
    else:
        out_specs = (y_spec, st_spec)
        out_shape = (y_shape, jax.ShapeDtypeStruct((n_seq, 2 * N_PAIR, LANE, LANE), F32))
    return pl.pallas_call(
        functools.partial(_hgrn_kernel, L=L, has_s0=has_s0),
        grid=(n_seq,),
        in_specs=in_specs,
        out_specs=out_specs,
        out_shape=out_shape,
        scratch_shapes=[pltpu.VMEM((L, 2 * BRANCH_WIDTH), F32),
                        pltpu.VMEM((L, 2 * BRANCH_WIDTH), F32),
                        pltpu.VMEM((L, BRANCH_WIDTH), F32),
                        pltpu.VMEM((2, L, BRANCH_WIDTH), F32),
                        pltpu.VMEM((2 * N_PAIR, LANE, LANE), F32),
                        pltpu.VMEM((2 * N_PAIR, HG, HGRN_CHUNK * LANE), BF16)],
        compiler_params=pltpu.CompilerParams(dimension_semantics=("arbitrary",),
                                             vmem_limit_bytes=VMEM_LIMIT),
        name="hgrn_lat" if has_s0 else "hgrn_ctx",
    )(*args)


def _hgrn_state_in(s):
    b = s.shape[0]
    st = jnp.swapaxes(s.astype(F32), -1, -2).reshape(b, 2, N_PAIR, 2, HD, HD)
    z = jnp.zeros_like(st[:, :, :, 0])
    top = jnp.concatenate([st[:, :, :, 0], z], axis=-1)
    bot = jnp.concatenate([z, st[:, :, :, 1]], axis=-1)
    return jnp.concatenate([top, bot], axis=-2).reshape(b, 2 * N_PAIR, LANE, LANE)


def _hgrn_state_out(s):
    b = s.shape[0]
    s = s.reshape(b, 2, N_PAIR, LANE, LANE)
    blocks = jnp.stack([s[..., :HD, :HD], s[..., HD:, HD:]], axis=3)
    return jnp.swapaxes(blocks, -1, -2).reshape(b, 2, C_HEADS, HD, HD)


def _pcol(p, name):
    o, w = P_OFF[name]
    return p[..., o:o + w]


def _pack_w_in(w_in):
    parts = []
    used = 0
    for name in _P_ORDER:
        o, w = REF_OFF[name]
        parts.append(w_in[..., o:o + w])
        pw = -(-w // LANE) * LANE
        if pw != w:
            parts.append(jnp.zeros(w_in.shape[:-1] + (pw - w,), w_in.dtype))
        used += pw
    parts.append(jnp.zeros(w_in.shape[:-1] + (N_P - used,), w_in.dtype))
    return jnp.concatenate(parts, axis=-1).astype(BF16)


def kernel(x_prompt, x_sample, cache_attn_k, cache_attn_v, cache_na_k, cache_na_v, state_delta, state_hgrn,
           c, c_ctx, norm_w, ada_w, ada_b, w_in, attn_sink, delta_conv, delta_a_log, delta_dt_bias,
           delta_norm_w, hgrn_lb, hgrn_norm_w, na_rpb, w_branch, w_out, mlp_w1, mlp_w2, final_norm_w):
    lb = jnp.cumsum(jax.nn.softmax(hgrn_lb.astype(F32), axis=0), axis=0)
    lb = lb - lb[:1]

    cvec = jnp.concatenate([c_ctx[None, :], c, jnp.zeros((16 - N_MOD_ROWS, D_MODEL), F32)], axis=0)
    mod = _adaln(cvec, ada_w, ada_b).reshape(DEPTH, 16, 6, D_MODEL)

    w_in_p = _pack_w_in(w_in)
    wb = w_branch.astype(BF16)
    wo = w_out.astype(BF16)
    w1 = mlp_w1.astype(BF16)
    w2 = mlp_w2.astype(BF16)
    fw = final_norm_w.reshape(1, D_MODEL)

    cos, sin = _rope_tables()
    sel = _hgrn_sel()
    cak = cache_attn_k.reshape(DEC_BATCH, DEPTH, PAST_LEN, A_KV_HEADS * HD)
    cav = cache_attn_v.reshape(DEC_BATCH, DEPTH, PAST_LEN, A_KV_HEADS * HD)
    cnk = cache_na_k.reshape(DEC_BATCH, DEPTH, PAST_LEN, BRANCH_WIDTH)
    cnv = cache_na_v.reshape(DEC_BATCH, DEPTH, PAST_LEN, BRANCH_WIDTH)
    lat_blk0 = N_CTX_TOK // DEC_SEQ

    x = jnp.concatenate([x_prompt.reshape(N_CTX_TOK, D_MODEL), x_sample.reshape(N_LAT_TOK, D_MODEL)], axis=0)
    ak_l, av_l, nk_l, nv_l, sd_l, sh_l = [], [], [], [], [], []
    for l in range(DEPTH):
        p = _inproj(x, mod[l], norm_w[l, 0].reshape(1, D_MODEL), w_in_p, l)
        pc = p[:N_CTX_TOK]
        ak_l.append(_pcol(pc, "a_k").reshape(BATCH, SEQ, A_KV_HEADS, HD))
        av_l.append(_pcol(pc, "a_v").reshape(BATCH, SEQ, A_KV_HEADS, HD))
        nk_l.append(_pcol(pc, "d_k").reshape(BATCH, SEQ, D_HEADS, HD))
        nv_l.append(_pcol(pc, "d_v").reshape(BATCH, SEQ, D_HEADS, HD))

        ya_c, yd_c = _ctx_attn(p, attn_sink[l])
        ya_l = _win_attn(p, attn_sink[l], cak, cav, l, cos, sin)
        yd_l = _na_attn(p, cnk, cnv, l, _na_bias_table(na_rpb[l]))

        prm, dnw = _delta_params(delta_a_log[l], delta_dt_bias[l], delta_norm_w[l])
        yb_c, sd = _delta(p, delta_conv[l], prm, dnw, None, L=SEQ, n_seq=BATCH, row_block0=0)
        yb_l = _delta(p, delta_conv[l], prm, dnw, state_delta[:, l].reshape(DEC_BATCH, N_HD, HD, HD),
                      L=DEC_SEQ, n_seq=DEC_BATCH, row_block0=lat_blk0)
        sd_l.append(sd.reshape(BATCH, 2, B_HEADS, HD, HD))

        lbl = lb[l].reshape(1, 2 * BRANCH_WIDTH)
        hnw = jnp.tile(hgrn_norm_w[l].astype(F32), C_HEADS).reshape(1, BRANCH_WIDTH)
        yc_c, sh = _hgrn(p, lbl, hnw, sel, None, L=SEQ, n_seq=BATCH, row_block0=0)
        yc_l = _hgrn(p, lbl, hnw, sel, _hgrn_state_in(state_hgrn[:, l]),
                     L=DEC_SEQ, n_seq=DEC_BATCH, row_block0=lat_blk0)
        sh_l.append(_hgrn_state_out(sh))

        x = _merge(x, [(ya_c, ya_l), (yb_c, yb_l), (yc_c, yc_l), (yd_c, yd_l)], p, mod[l], wb[l], wo[l])
        x = _mlp(x, mod[l], norm_w[l, 1].reshape(1, D_MODEL), w1[l], w2[l], fw, final=(l == DEPTH - 1))

    y_prompt = x[:N_CTX_TOK].reshape(BATCH, SEQ, D_MODEL)
    y_sample = x[N_CTX_TOK:].reshape(DEC_BATCH, DEC_SEQ, D_MODEL)
    return (y_prompt, y_sample, jnp.stack(ak_l, axis=1), jnp.stack(av_l, axis=1), jnp.stack(nk_l, axis=1),
            jnp.stack(nv_l, axis=1), jnp.stack(sd_l, axis=1), jnp.stack(sh_l, axis=1))
```
